```python
import math
import jax
import jax.numpy as jnp
from jax import lax
import numpy as np

D_MODEL = 1024
BATCH = 4
SEQ = 8192
DEPTH = 4

GRID_W = 64
CTX_LEN = 256
ROPE_BASE = 10000.0
N_MOD = 6

MLA_HEADS = 8
MLA_Q_RANK = 256
MLA_KV_RANK = 128
MLA_NOPE = 64
MLA_ROPE = 32
MLA_V = 64
MLA_SCALE = (MLA_NOPE + MLA_ROPE) ** -0.5
Q_BLOCK = 128

RET_HEADS = 8
RET_DK = 64
RET_DV = 64
RET_CHUNK = 128

DN_HEADS = 8
DN_DK = 128
DN_DV = 128
DN_CONV = 5
DN_CHUNK = 64

PEER_HEADS = 8
PEER_KEYS = 128
PEER_N = PEER_KEYS * PEER_KEYS
PEER_QDIM = 256
PEER_TOPK = 16
PEER_BLOCK = 128

DEEPNORM_ALPHA = (2 * DEPTH) ** 0.25
DEEPNORM_BETA = (8 * DEPTH) ** -0.25
N_EVEN = (DEPTH + 1) // 2
N_ODD = DEPTH // 2

AR_SIZES = (MLA_Q_RANK, MLA_KV_RANK, MLA_ROPE, RET_HEADS * RET_DK, RET_HEADS * RET_DK, RET_HEADS * RET_DV, RET_HEADS * RET_DV)
AR_IN = MLA_Q_RANK + MLA_KV_RANK + MLA_ROPE + 2 * RET_HEADS * RET_DK + 2 * RET_HEADS * RET_DV
AR_OUT = MLA_HEADS * MLA_V + RET_HEADS * RET_DV
DN_SIZES = (DN_HEADS * DN_DK, DN_HEADS * DN_DK, DN_HEADS * DN_DV, DN_HEADS * DN_DV, 2 * DN_HEADS, 2 * DN_HEADS)
DN_IN = 2 * DN_HEADS * DN_DK + 2 * DN_HEADS * DN_DV + 4 * DN_HEADS
DN_CONV_CH = 2 * DN_HEADS * DN_DK + DN_HEADS * DN_DV
DN_OUT = DN_HEADS * DN_DV

kernel_name = 'hybrid_mla_retention_gdn_peer_dit'

F32 = jnp.float32


def split_cols(z, sizes):
    return jnp.split(z, [int(i) for i in np.cumsum(sizes)[:-1]], axis=-1)


def layer_norm(x, g, b, eps=1e-5):
    xf = x.astype(F32)
    xc = xf - jnp.mean(xf, -1, keepdims=True)
    var = jnp.mean(xc * xc, -1, keepdims=True)
    return (xc * lax.rsqrt(var + eps) * g + b).astype(x.dtype)


def rms_norm(x, g, eps=1e-6):
    xf = x.astype(F32)
    return (xf * lax.rsqrt(jnp.mean(xf * xf, -1, keepdims=True) + eps) * g).astype(x.dtype)


def l2_norm(x, eps=1e-6):
    xf = x.astype(F32)
    return (xf * lax.rsqrt(jnp.sum(xf * xf, -1, keepdims=True) + eps)).astype(x.dtype)


def group_norm_heads(o, g, eps=1e-5):
    of = o.astype(F32)
    oc = of - jnp.mean(of, -1, keepdims=True)
    y = oc * lax.rsqrt(jnp.mean(oc * oc, -1, keepdims=True) + eps)
    b, h, L, dv = o.shape
    return (jnp.transpose(y, (0, 2, 1, 3)).reshape(b, L, h * dv) * g).astype(o.dtype)


def modulate(x, shift, scale):
    return x * (1 + scale) + shift


def flip_seq(t):
    return jnp.flip(t, axis=2)


def axial_rope(length, dim):
    rows = length // GRID_W
    n_freq = dim // 4
    inv = ROPE_BASE ** (-jnp.arange(n_freq, dtype=F32) / n_freq)
    row = jnp.repeat(jnp.arange(rows, dtype=F32), GRID_W)
    col = jnp.tile(jnp.arange(GRID_W, dtype=F32), rows)
    ang = jnp.concatenate([row[:, None] * inv, col[:, None] * inv], axis=-1)
    return jnp.cos(ang), jnp.sin(ang)


def apply_rope(x, cos, sin):
    x1, x2 = jnp.split(x, 2, axis=-1)
    return jnp.concatenate([x1 * cos - x2 * sin, x2 * cos + x1 * sin], axis=-1).astype(x.dtype)


def mla_heads(cq, ckv, kr, q_norm, w_uq, kv_norm, w_ukv, rope):
    b, L, _ = cq.shape
    q = (rms_norm(cq, q_norm) @ w_uq).reshape(b, L, MLA_HEADS, MLA_NOPE + MLA_ROPE)
    kv = (rms_norm(ckv, kv_norm) @ w_ukv).reshape(b, L, MLA_HEADS, MLA_NOPE + MLA_V)
    qn, qr = q[..., :MLA_NOPE], q[..., MLA_NOPE:]
    kn, v = kv[..., :MLA_NOPE], kv[..., MLA_NOPE:]
    if rope is not None:
        cos, sin = rope
        qr = apply_rope(qr, cos[:, None], sin[:, None])
        kr = apply_rope(kr, cos, sin)
    return qn, qr, kn, kr, v


def softmax_attend(qn, qr, kn, kr, v):
    s = jnp.einsum('bqhd,bkhd->bhqk', qn, kn) + jnp.einsum('bqhr,bkr->bhqk', qr, kr)
    p = jax.nn.softmax(s.astype(F32) * MLA_SCALE, axis=-1)
    return jnp.einsum('bhqk,bkhd->bqhd', p.astype(v.dtype), v)


def latent_attention(qn, qr, kn, kr, v):
    b, L, h, _ = qn.shape
    nb = L // Q_BLOCK
    def blocks(t):
        return jnp.swapaxes(t.reshape((b, nb, Q_BLOCK) + t.shape[2:]), 0, 1)
    o = lax.map(lambda qs: softmax_attend(qs[0], qs[1], kn, kr, v), (blocks(qn), blocks(qr)))
    return jnp.swapaxes(o, 0, 1).reshape(b, L, h * MLA_V)


def retention_dir(q, k, v, log_gamma, state0, strict):
    b, h, L, dk = q.shape
    dv = v.shape[-1]
    C = RET_CHUNK
    n = L // C
    qc = q.astype(F32).reshape(b, h, n, C, dk)
    kc = k.astype(F32).reshape(b, h, n, C, dk)
    vc = v.astype(F32).reshape(b, h, n, C, dv)
    pos = jnp.arange(C, dtype=F32)
    dist = pos[:, None] - pos[None, :]
    mask = (dist > 0) if strict else (dist >= 0)
    dmat = jnp.where(mask, jnp.exp(jnp.where(mask, dist, 0.0) * log_gamma[:, None, None]), 0.0)
    scores = jnp.einsum('bhncd,bhnsd->bhncs', qc, kc) * dmat[:, None]
    intra = jnp.einsum('bhncs,bhnsv->bhncv', scores, vc)
    q_dec = jnp.exp((pos + 1.0) * log_gamma[:, None])
    k_dec = jnp.exp((C - 1.0 - pos) * log_gamma[:, None])
    chunk_dec = jnp.exp(C * log_gamma)[None, :, None, None]
    kv = jnp.einsum('bhncd,hc,bhncv->nbhdv', kc, k_dec, vc)
    def step(s, kv_n):
        return s * chunk_dec + kv_n, s
    s_final, s_start = lax.scan(step, state0, kv)
    inter = jnp.einsum('bhncd,hc,nbhdv->bhncv', qc, q_dec, s_start)
    return (intra + inter).reshape(b, h, L, dv), s_final


def ret_heads(rq, rk, rv, rope):
    b, L, _ = rq.shape
    q = rq.reshape(b, L, RET_HEADS, RET_DK)
    k = rk.reshape(b, L, RET_HEADS, RET_DK)
    if rope is not None:
        cos, sin = rope
        q = apply_rope(q, cos[:, None], sin[:, None])
        k = apply_rope(k, cos[:, None], sin[:, None])
    k = k * RET_DK ** -0.5
    v = rv.reshape(b, L, RET_HEADS, RET_DV)
    return jnp.transpose(q, (0, 2, 1, 3)), jnp.transpose(k, (0, 2, 1, 3)), jnp.transpose(v, (0, 2, 1, 3))


def attn_retention_mixer(hl, hc, w_in, q_norm, w_uq, kv_norm, w_ukv, gn_g, w_out, rope_mla, rope_ret, ctx_out):
    zl = split_cols(hl @ w_in, AR_SIZES)
    zc = split_cols(hc @ w_in, AR_SIZES)
    qn_l, qr_l, kn_l, kr_l, v_l = mla_heads(zl[0], zl[1], zl[2], q_norm, w_uq, kv_norm, w_ukv, rope_mla)
    qn_c, qr_c, kn_c, kr_c, v_c = mla_heads(zc[0], zc[1], zc[2], q_norm, w_uq, kv_norm, w_ukv, None)
    kn_all = jnp.concatenate([kn_c, kn_l], axis=1)
    kr_all = jnp.concatenate([kr_c, kr_l], axis=1)
    v_all = jnp.concatenate([v_c, v_l], axis=1)
    mla_l = latent_attention(qn_l, qr_l, kn_all, kr_all, v_all)
    log_gamma = jnp.log1p(-jnp.exp2(-5.0 - jnp.arange(RET_HEADS, dtype=F32)))
    ql, kl, vl = ret_heads(zl[3], zl[4], zl[5], rope_ret)
    qc, kc, vc = ret_heads(zc[3], zc[4], zc[5], None)
    zero = jnp.zeros((hl.shape[0], RET_HEADS, RET_DK, RET_DV), F32)
    oc_f, s_f = retention_dir(qc, kc, vc, log_gamma, zero, False)
    oc_b, s_b = retention_dir(flip_seq(qc), flip_seq(kc), flip_seq(vc), log_gamma, zero, True)
    ol_f, _ = retention_dir(ql, kl, vl, log_gamma, s_f, False)
    ol_b, _ = retention_dir(flip_seq(ql), flip_seq(kl), flip_seq(vl), log_gamma, s_b, True)
    ret_l = group_norm_heads(ol_f + flip_seq(ol_b), gn_g) * jax.nn.silu(zl[6])
    yl = jnp.concatenate([mla_l, ret_l], axis=-1) @ w_out
    if not ctx_out:
        return yl, None
    b, Lc = hc.shape[0], hc.shape[1]
    mla_c = softmax_attend(qn_c, qr_c, kn_c, kr_c, v_c).reshape(b, Lc, MLA_HEADS * MLA_V)
    ret_c = group_norm_heads(oc_f + flip_seq(oc_b), gn_g) * jax.nn.silu(zc[6])
    yc = jnp.concatenate([mla_c, ret_c], axis=-1) @ w_out
    return yl, yc


def gated_delta_dir(q, k, v, g, beta, state0):
    b, h, L, dk = q.shape
    dv = v.shape[-1]
    C = DN_CHUNK
    n = L // C
    qc = q.astype(F32).reshape(b, h, n, C, dk)
    kc = k.astype(F32).reshape(b, h, n, C, dk)
    vc = v.astype(F32).reshape(b, h, n, C, dv)
    gc = jnp.cumsum(g.astype(F32).reshape(b, h, n, C), axis=-1)
    bc = beta.astype(F32).reshape(b, h, n, C, 1)
    pos = jnp.arange(C)
    incl = pos[:, None] >= pos[None, :]
    strict = pos[:, None] > pos[None, :]
    gdiff = gc[..., :, None] - gc[..., None, :]
    decay = jnp.where(incl, jnp.exp(jnp.where(incl, gdiff, 0.0)), 0.0)
    kb = kc * bc
    lmat = jnp.where(strict, jnp.einsum('bhncd,bhnsd->bhncs', kb, kc) * decay, 0.0)
    eye = jnp.eye(C, dtype=F32)
    t_inv = lax.linalg.triangular_solve(eye + lmat, jnp.broadcast_to(eye, lmat.shape), left_side=True, lower=True, unit_diagonal=True)
    u = jnp.einsum('bhncs,bhnsv->bhncv', t_inv, vc * bc)
    w = jnp.einsum('bhncs,bhnsd->bhncd', t_inv, kb * jnp.exp(gc)[..., None])
    attn = jnp.einsum('bhncd,bhnsd->bhncs', qc, kc) * decay
    q_dec = qc * jnp.exp(gc)[..., None]
    k_dec = kc * jnp.exp(gc[..., -1:] - gc)[..., None]
    chunk_dec = jnp.exp(gc[..., -1])
    xs = tuple(jnp.moveaxis(t, 2, 0) for t in (attn, q_dec, k_dec, u, w, chunk_dec))
    def step(s, inp):
        a_n, qd_n, kd_n, u_n, w_n, cd_n = inp
        v_new = u_n - jnp.einsum('bhcd,bhdv->bhcv', w_n, s)
        o_n = jnp.einsum('bhcd,bhdv->bhcv', qd_n, s) + jnp.einsum('bhcs,bhsv->bhcv', a_n, v_new)
        s = s * cd_n[..., None, None] + jnp.einsum('bhcd,bhcv->bhdv', kd_n, v_new)
        return s, o_n
    s_final, o = lax.scan(step, state0, xs)
    return jnp.moveaxis(o, 0, 2).reshape(b, h, L, dv), s_final


def short_conv(x, w):
    K = w.shape[0]
    return lax.conv_general_dilated(x, w.astype(x.dtype)[:, None, :], window_strides=(1,), padding=[((K - 1) // 2, K // 2)], dimension_numbers=('NWC', 'WIO', 'NWC'), feature_group_count=x.shape[-1])


def dn_heads(z, conv_w, a_log, dt_bias):
    b, L, _ = z.shape
    q, k, v, gate, a, bt = split_cols(z, DN_SIZES)
    qkv = jax.nn.silu(short_conv(jnp.concatenate([q, k, v], axis=-1), conv_w))
    q, k, v = split_cols(qkv, DN_SIZES[:3])
    q = l2_norm(q.reshape(b, L, DN_HEADS, DN_DK)) * DN_DK ** -0.5
    k = l2_norm(k.reshape(b, L, DN_HEADS, DN_DK))
    v = v.reshape(b, L, DN_HEADS, DN_DV)
    a = a.reshape(b, L, 2, DN_HEADS).astype(F32)
    g = -jnp.exp(a_log.astype(F32)) * jax.nn.softplus(a + dt_bias.astype(F32))
    beta = jax.nn.sigmoid(bt.reshape(b, L, 2, DN_HEADS).astype(F32))
    tr = lambda t: jnp.transpose(t, (0, 2, 1, 3))
    return tr(q), tr(k), tr(v), jnp.transpose(g, (2, 0, 3, 1)), jnp.transpose(beta, (2, 0, 3, 1)), gate


def gated_out(o, gate, norm_g):
    b, h, L, dv = o.shape
    y = rms_norm(jnp.transpose(o, (0, 2, 1, 3)), norm_g) * jax.nn.silu(gate.reshape(b, L, h, dv))
    return y.reshape(b, L, h * dv)


def deltanet_mixer(hl, hc, w_in, conv_w, a_log, dt_bias, norm_g, w_out, ctx_out):
    ql, kl, vl, gl, bl, zl = dn_heads(hl @ w_in, conv_w, a_log, dt_bias)
    qc, kc, vc, gc, bc, zc = dn_heads(hc @ w_in, conv_w, a_log, dt_bias)
    zero = jnp.zeros((hl.shape[0], DN_HEADS, DN_DK, DN_DV), F32)
    oc_f, s_f = gated_delta_dir(qc, kc, vc, gc[0], bc[0], zero)
    oc_b, s_b = gated_delta_dir(flip_seq(qc), flip_seq(kc), flip_seq(vc), flip_seq(gc[1]), flip_seq(bc[1]), zero)
    ol_f, _ = gated_delta_dir(ql, kl, vl, gl[0], bl[0], s_f)
    ol_b, _ = gated_delta_dir(flip_seq(ql), flip_seq(kl), flip_seq(vl), flip_seq(gl[1]), flip_seq(bl[1]), s_b)
    yl = gated_out(ol_f + flip_seq(ol_b), zl, norm_g) @ w_out
    if not ctx_out:
        return yl, None
    yc = gated_out(oc_f + flip_seq(oc_b), zc, norm_g) @ w_out
    return yl, yc


def peer(h, w_q, k1, k2, u_tab, v_tab):
    b, L, d = h.shape
    blocks = h.reshape(b * L // PEER_BLOCK, PEER_BLOCK, d)
    half = PEER_QDIM // 2
    def block(xb):
        q = (xb @ w_q).reshape(PEER_BLOCK, PEER_HEADS, 2, half)
        s1 = jnp.einsum('phd,hnd->phn', q[:, :, 0], k1)
        s2 = jnp.einsum('phd,hnd->phn', q[:, :, 1], k2)
        v1, i1 = lax.top_k(s1, PEER_TOPK)
        v2, i2 = lax.top_k(s2, PEER_TOPK)
        cand = (v1[..., :, None] + v2[..., None, :]).reshape(PEER_BLOCK, PEER_HEADS, PEER_TOPK * PEER_TOPK)
        cand_idx = (i1[..., :, None] * PEER_KEYS + i2[..., None, :]).reshape(PEER_BLOCK, PEER_HEADS, PEER_TOPK * PEER_TOPK)
        sc, j = lax.top_k(cand, PEER_TOPK)
        e_idx = jnp.take_along_axis(cand_idx, j, axis=-1)
        gate = jax.nn.softmax(sc.astype(F32), axis=-1)
        act = jax.nn.gelu(jnp.einsum('phkd,pd->phk', u_tab[e_idx], xb).astype(F32), approximate=False)
        return jnp.einsum('phk,phkd->pd', (gate * act).astype(xb.dtype), v_tab[e_idx])
    return lax.map(block, blocks).reshape(b, L, d)


def setup_inputs(seed: int = 0) -> dict:
    key = jax.random.key(seed)
    ks = iter(jax.random.split(key, 32))
    D = D_MODEL
    def nrm(shape, s):
        return jax.random.normal(next(ks), shape, F32) * s
    dt = jnp.exp(jax.random.uniform(next(ks), (N_ODD, 2, DN_HEADS), F32, minval=math.log(1e-3), maxval=math.log(1e-1)))
    return {
        'x': nrm((BATCH, SEQ, D), 1.0),
        'c': nrm((BATCH, D), 1.0),
        'ctx': nrm((BATCH, CTX_LEN, D), 1.0),
        'c_ctx': nrm((D,), 1.0),
        'ada_w': nrm((DEPTH, D, N_MOD * D), 0.5 * D ** -0.5),
        'ada_b': nrm((DEPTH, N_MOD * D), 0.02),
        'ln1_g': 1.0 + nrm((DEPTH, D), 0.02),
        'ln1_b': nrm((DEPTH, D), 0.02),
        'ln2_g': 1.0 + nrm((DEPTH, D), 0.02),
        'ln2_b': nrm((DEPTH, D), 0.02),
        'ar_w_in': nrm((N_EVEN, D, AR_IN), D ** -0.5),
        'mla_q_norm': 1.0 + nrm((N_EVEN, MLA_Q_RANK), 0.02),
        'mla_w_uq': nrm((N_EVEN, MLA_Q_RANK, MLA_HEADS * (MLA_NOPE + MLA_ROPE)), MLA_Q_RANK ** -0.5),
        'mla_kv_norm': 1.0 + nrm((N_EVEN, MLA_KV_RANK), 0.02),
        'mla_w_ukv': nrm((N_EVEN, MLA_KV_RANK, MLA_HEADS * (MLA_NOPE + MLA_V)), MLA_KV_RANK ** -0.5),
        'ret_gn_g': 1.0 + nrm((N_EVEN, RET_HEADS * RET_DV), 0.02),
        'ar_w_out': nrm((N_EVEN, AR_OUT, D), AR_OUT ** -0.5 * DEEPNORM_BETA),
        'dn_w_in': nrm((N_ODD, D, DN_IN), D ** -0.5),
        'dn_conv': nrm((N_ODD, DN_CONV, DN_CONV_CH), DN_CONV ** -0.5),
        'dn_a_log': jnp.log(jax.random.uniform(next(ks), (N_ODD, 2, DN_HEADS), F32, minval=1.0, maxval=16.0)),
        'dn_dt_bias': dt + jnp.log(-jnp.expm1(-dt)),
        'dn_norm_g': 1.0 + nrm((N_ODD, DN_DV), 0.02),
        'dn_w_out': nrm((N_ODD, DN_OUT, D), DN_OUT ** -0.5 * DEEPNORM_BETA),
        'peer_w_q': nrm((DEPTH, D, PEER_HEADS * PEER_QDIM), D ** -0.5),
        'peer_k1': nrm((DEPTH, PEER_HEADS, PEER_KEYS, PEER_QDIM // 2), (PEER_QDIM // 2) ** -0.5),
        'peer_k2': nrm((DEPTH, PEER_HEADS, PEER_KEYS, PEER_QDIM // 2), (PEER_QDIM // 2) ** -0.5),
        'peer_u': nrm((DEPTH, PEER_N, D), D ** -0.5),
        'peer_v': nrm((DEPTH, PEER_N, D), DEEPNORM_BETA * PEER_HEADS ** -0.5),
    }


def reference(x, c, ctx, c_ctx, ada_w, ada_b, ln1_g, ln1_b, ln2_g, ln2_b,
              ar_w_in, mla_q_norm, mla_w_uq, mla_kv_norm, mla_w_ukv, ret_gn_g, ar_w_out,
              dn_w_in, dn_conv, dn_a_log, dn_dt_bias, dn_norm_g, dn_w_out,
              peer_w_q, peer_k1, peer_k2, peer_u, peer_v):
    L = x.shape[1]
    rope_mla = axial_rope(L, MLA_ROPE)
    rope_ret = axial_rope(L, RET_DK)
    xl, xc = x, ctx
    for l in range(DEPTH):
        last = l == DEPTH - 1
        j = l // 2
        mod_l = (jax.nn.silu(c) @ ada_w[l] + ada_b[l])[:, None, :]
        mod_c = (jax.nn.silu(c_ctx) @ ada_w[l] + ada_b[l])[None, None, :]
        sh1l, sc1l, g1l, sh2l, sc2l, g2l = jnp.split(mod_l, N_MOD, axis=-1)
        sh1c, sc1c, g1c, sh2c, sc2c, g2c = jnp.split(mod_c, N_MOD, axis=-1)
        hl = modulate(xl, sh1l, sc1l)
        hc = modulate(xc, sh1c, sc1c)
        if l % 2 == 0:
            yl, yc = attn_retention_mixer(hl, hc, ar_w_in[j], mla_q_norm[j], mla_w_uq[j], mla_kv_norm[j], mla_w_ukv[j], ret_gn_g[j], ar_w_out[j], rope_mla, rope_ret, not last)
        else:
            yl, yc = deltanet_mixer(hl, hc, dn_w_in[j], dn_conv[j], dn_a_log[j], dn_dt_bias[j], dn_norm_g[j], dn_w_out[j], not last)
        xl = layer_norm(DEEPNORM_ALPHA * xl + g1l * yl, ln1_g[l], ln1_b[l])
        fl = peer(modulate(xl, sh2l, sc2l), peer_w_q[l], peer_k1[l], peer_k2[l], peer_u[l], peer_v[l])
        xl = layer_norm(DEEPNORM_ALPHA * xl + g2l * fl, ln2_g[l], ln2_b[l])
        if not last:
            xc = layer_norm(DEEPNORM_ALPHA * xc + g1c * yc, ln1_g[l], ln1_b[l])
            fc = peer(modulate(xc, sh2c, sc2c), peer_w_q[l], peer_k1[l], peer_k2[l], peer_u[l], peer_v[l])
            xc = layer_norm(DEEPNORM_ALPHA * xc + g2c * fc, ln2_g[l], ln2_b[l])
    return xl
```

```python
import functools
import math

import numpy as np
import jax
import jax.numpy as jnp
from jax import lax
from jax.experimental import pallas as pl
from jax.experimental.pallas import tpu as pltpu

F32 = jnp.float32
BF16 = jnp.bfloat16

DEPTH = 4
GRID_W = 64
ROPE_BASE = 10000.0
N_MOD = 6

MLA_HEADS = 8
MLA_Q_RANK = 256
MLA_KV_RANK = 128
MLA_NOPE = 64
MLA_ROPE = 32
MLA_V = 64
MLA_SCALE = (MLA_NOPE + MLA_ROPE) ** -0.5

RET_HEADS = 8
RET_DK = 64
RET_DV = 64
RET_CHUNK = 128

DN_HEADS = 8
DN_DK = 128
DN_DV = 128
DN_CONV = 5
DN_CHUNK = 64

PEER_HEADS = 8
PEER_KEYS = 128
PEER_QDIM = 256
PEER_TOPK = 16

DEEPNORM_ALPHA = (2 * DEPTH) ** 0.25

LANE = 128
VMEM_LIMIT = 56 * 1024 * 1024


def _cparams(sem):
    return pltpu.CompilerParams(dimension_semantics=sem, vmem_limit_bytes=VMEM_LIMIT)


def _ada_kernel(c_ref, w_ref, b_ref, o_ref):
    c = c_ref[...]
    a = (c * jax.nn.sigmoid(c)).astype(BF16)
    o_ref[...] = jnp.dot(a, w_ref[...].astype(BF16), preferred_element_type=F32) + b_ref[...]


def ada_all(cc, ada_w, ada_b, tn=1024):
    depth, d, n = ada_w.shape
    m = cc.shape[0]
    return pl.pallas_call(
        _ada_kernel,
        grid=(depth, n // tn),
        in_specs=[
            pl.BlockSpec((m, d), lambda l, j: (0, 0)),
            pl.BlockSpec((None, d, tn), lambda l, j: (l, 0, j)),
            pl.BlockSpec((None, 1, tn), lambda l, j: (l, 0, j)),
        ],
        out_specs=pl.BlockSpec((None, m, tn), lambda l, j: (l, 0, j)),
        out_shape=jax.ShapeDtypeStruct((depth, m, n), F32),
        compiler_params=_cparams(("arbitrary", "arbitrary")),
        name="ada",
    )(cc, ada_w, ada_b.reshape(depth, 1, n))


def _proj_kernel(*refs, mode, splits, eps):
    if mode == "mod":
        x_ref, sh_ref, sc_ref, w_ref = refs[:4]
        outs = refs[4:]
        x = x_ref[...] * (1.0 + sc_ref[...]) + sh_ref[...]
    elif mode == "rms":
        x_ref, g_ref, w_ref = refs[:3]
        outs = refs[3:]
        x = x_ref[...]
        x = x * lax.rsqrt(jnp.mean(x * x, axis=-1, keepdims=True) + eps) * g_ref[...]
    else:
        x_ref, w_ref = refs[:2]
        outs = refs[2:]
        x = x_ref[...]
    z = jnp.dot(x.astype(BF16), w_ref[...], preferred_element_type=F32)
    for o_ref, (a, b) in zip(outs, splits):
        o_ref[...] = z[:, a:b].astype(o_ref.dtype)


def proj(x, w, splits, *, mode="none", shift=None, scale=None, gain=None, ctx_blocks=1, tm=256,
         eps=1e-6, name="proj"):
    bsz, t, k = x.shape
    n = w.shape[1]
    row = lambda b, j: (b, j, 0)
    sel = lambda b, j: (b, jnp.minimum(j // ctx_blocks, 1), 0, 0)
    in_specs = [pl.BlockSpec((None, tm, k), row)]
    args = [x]
    if mode == "mod":
        in_specs += [pl.BlockSpec((None, None, 1, k), sel)] * 2
        args += [shift, scale]
    elif mode == "rms":
        in_specs += [pl.BlockSpec((1, k), lambda b, j: (0, 0))]
        args += [gain.reshape(1, k)]
    in_specs += [pl.BlockSpec((k, n), lambda b, j: (0, 0))]
    args += [w]
    return pl.pallas_call(
        functools.partial(_proj_kernel, mode=mode, splits=tuple(splits), eps=eps),
        grid=(bsz, t // tm),
        in_specs=in_specs,
        out_specs=[pl.BlockSpec((None, tm, b - a), row) for a, b in splits],
        out_shape=[jax.ShapeDtypeStruct((bsz, t, b - a), F32) for a, b in splits],
        compiler_params=_cparams(("parallel", "parallel")),
        name=name,
    )(*args)


def _resid_ln_kernel(x_ref, y_ref, gate_ref, g_ref, b_ref, o_ref, *, eps):
    v = DEEPNORM_ALPHA * x_ref[...] + gate_ref[...] * y_ref[...]
    vc = v - jnp.mean(v, axis=-1, keepdims=True)
    var = jnp.mean(vc * vc, axis=-1, keepdims=True)
    o_ref[...] = vc * lax.rsqrt(var + eps) * g_ref[...] + b_ref[...]


def resid_ln(x, y, gate, g, b, *, ctx_blocks=1, tm=256, eps=1e-5):
    bsz, t, d = x.shape
    row = lambda b_, j: (b_, j, 0)
    sel = lambda b_, j: (b_, jnp.minimum(j // ctx_blocks, 1), 0, 0)
    vec = lambda b_, j: (0, 0)
    return pl.pallas_call(
        functools.partial(_resid_ln_kernel, eps=eps),
        grid=(bsz, t // tm),
        in_specs=[pl.BlockSpec((None, tm, d), row), pl.BlockSpec((None, tm, d), row),
                  pl.BlockSpec((None, None, 1, d), sel),
                  pl.BlockSpec((1, d), vec), pl.BlockSpec((1, d), vec)],
        out_specs=pl.BlockSpec((None, tm, d), row),
        out_shape=jax.ShapeDtypeStruct((bsz, t, d), F32),
        compiler_params=_cparams(("parallel", "parallel")),
        name="resid_ln",
    )(x, y, gate, g.reshape(1, d), b.reshape(1, d))


def _attn_kernel(q_ref, k_ref, v_ref, o_ref, *, tk, nk):
    q = q_ref[...]
    tq = q.shape[0]
    dv = v_ref.shape[-1]

    def body(i, carry):
        m, l, acc = carry
        start = pl.multiple_of(i * tk, tk)
        k = k_ref[pl.ds(start, tk), :]
        v = v_ref[pl.ds(start, tk), :]
        s = lax.dot_general(q, k, (((1,), (1,)), ((), ())), preferred_element_type=F32)
        m_new = jnp.maximum(m, jnp.max(s, axis=1, keepdims=True))
        p = jnp.exp(s - m_new)
        alpha = jnp.exp(m - m_new)
        l = alpha * l + jnp.sum(p, axis=1, keepdims=True)
        acc = alpha * acc + jnp.dot(p.astype(BF16), v, preferred_element_type=F32)
        return m_new, l, acc

    m0 = jnp.full((tq, 1), -jnp.inf, F32)
    l0 = jnp.zeros((tq, 1), F32)
    a0 = jnp.zeros((tq, dv), F32)
    _, l, acc = lax.fori_loop(0, nk, body, (m0, l0, a0))
    o_ref[...] = acc / l


def attention(q, k, v, *, tq, tk):
    bsz, h, lq, dq = q.shape
    lk, dv = v.shape[2], v.shape[3]
    return pl.pallas_call(
        functools.partial(_attn_kernel, tk=tk, nk=lk // tk),
        grid=(bsz, h, lq // tq),
        in_specs=[pl.BlockSpec((None, None, tq, dq), lambda b, hh, i: (b, hh, i, 0)),
                  pl.BlockSpec((None, None, lk, dq), lambda b, hh, i: (b, hh, 0, 0)),
                  pl.BlockSpec((None, None, lk, dv), lambda b, hh, i: (b, hh, 0, 0))],
        out_specs=pl.BlockSpec((None, None, tq, dv), lambda b, hh, i: (b, hh, i, 0)),
        out_shape=jax.ShapeDtypeStruct((bsz, h, lq, dv), F32),
        compiler_params=_cparams(("parallel", "parallel", "parallel")),
        name="mla_attn",
    )(q, k, v)


def _top_values(s, k):
    vals = []
    cur = s
    for i in range(k):
        m = jnp.max(cur, axis=0, keepdims=True)
        vals.append(m)
        if i + 1 < k:
            cur = jnp.where(cur >= m, -jnp.inf, cur)
    return vals


def _gelu_exact(x):
    return 0.5 * x * (1.0 + lax.erf(x * (2.0 ** -0.5)))


def _peer_kernel(x_ref, sh_ref, sc_ref, wqt_ref, k1_ref, k2_ref, u_ref, vt_ref, o_ref,
                 xb_ref, s1_ref, s2_ref, e1_ref, e2_ref, tau_ref, a_ref, acc_ref, *, rows):
    e = pl.program_id(2)
    ne = pl.num_programs(2)
    half = PEER_QDIM // 2
    nt = (((1,), (1,)), ((), ()))

    @pl.when(e == 0)
    def _prepare():
        x = x_ref[...] * (1.0 + sc_ref[...]) + sh_ref[...]
        xb = x.astype(BF16)
        xb_ref[...] = xb
        qt = lax.dot_general(wqt_ref[...], xb, nt, preferred_element_type=F32)
        for h in range(PEER_HEADS):
            q1 = qt[h * PEER_QDIM: h * PEER_QDIM + half].astype(BF16)
            q2 = qt[h * PEER_QDIM + half: (h + 1) * PEER_QDIM].astype(BF16)
            s1 = jnp.dot(k1_ref[h], q1, preferred_element_type=F32)
            s2 = jnp.dot(k2_ref[h], q2, preferred_element_type=F32)
            v1 = _top_values(s1, PEER_TOPK)
            v2 = _top_values(s2, PEER_TOPK)
            cands = [v1[a] + v2[b] for a in range(PEER_TOPK) for b in range(PEER_TOPK)
                     if (a + 1) * (b + 1) <= PEER_TOPK]
            top = cands[0]
            cur = list(cands)
            for i in range(PEER_TOPK):
                m = functools.reduce(jnp.maximum, cur)
                if i + 1 < PEER_TOPK:
                    cur = [jnp.where(c >= m, -jnp.inf, c) for c in cur]
            tau = m
            z = functools.reduce(
                jnp.add, [jnp.where(c >= tau, jnp.exp(c - top), 0.0) for c in cands])
            s1_ref[h] = s1
            s2_ref[h] = s2
            e1_ref[h] = jnp.exp(s1 - v1[0]) / z
            e2_ref[h] = jnp.exp(s2 - v2[0])
            tau_ref[h] = tau
        acc_ref[...] = jnp.zeros_like(acc_ref)

    ht = lax.dot_general(u_ref[...], xb_ref[...], nt, preferred_element_type=F32)
    r0 = e * rows
    tb = ht.shape[1]
    for r in range(rows):
        def head(h, w, r=r):
            s1row = s1_ref[h, pl.ds(r0 + r, 1), :]
            e1row = e1_ref[h, pl.ds(r0 + r, 1), :]
            sums = s1row + s2_ref[h]
            return w + jnp.where(sums >= tau_ref[h], e1row * e2_ref[h], 0.0)
        w = lax.fori_loop(0, PEER_HEADS, head, jnp.zeros((PEER_KEYS, tb), F32))
        hr = ht[r * PEER_KEYS:(r + 1) * PEER_KEYS]
        a_ref[r * PEER_KEYS:(r + 1) * PEER_KEYS, :] = (w * _gelu_exact(hr)).astype(BF16)
    acc_ref[...] += jnp.dot(vt_ref[...], a_ref[...], preferred_element_type=F32)

    @pl.when(e == ne - 1)
    def _finish():
        o_ref[...] = acc_ref[...].T


def peer(x, shift, scale, wqt, k1, k2, u, vt, *, ctx_blocks, tb=256, eb=1024):
    bsz, t, d = x.shape
    n = u.shape[0]
    rows = eb // PEER_KEYS
    row = lambda b, j, e: (b, j, 0)
    sel = lambda b, j, e: (b, jnp.minimum(j // ctx_blocks, 1), 0, 0)
    const2 = lambda b, j, e: (0, 0)
    const3 = lambda b, j, e: (0, 0, 0)
    tab = pltpu.VMEM((PEER_HEADS, PEER_KEYS, tb), F32)
    return pl.pallas_call(
        functools.partial(_peer_kernel, rows=rows),
        grid=(bsz, t // tb, n // eb),
        in_specs=[pl.BlockSpec((None, tb, d), row),
                  pl.BlockSpec((None, None, 1, d), sel), pl.BlockSpec((None, None, 1, d), sel),
                  pl.BlockSpec(wqt.shape, const2),
                  pl.BlockSpec(k1.shape, const3), pl.BlockSpec(k2.shape, const3),
                  pl.BlockSpec((eb, d), lambda b, j, e: (e, 0)),
                  pl.BlockSpec((d, eb), lambda b, j, e: (0, e))],
        out_specs=pl.BlockSpec((None, tb, d), row),
        out_shape=jax.ShapeDtypeStruct((bsz, t, d), F32),
        scratch_shapes=[pltpu.VMEM((tb, d), BF16), tab, tab, tab, tab,
                        pltpu.VMEM((PEER_HEADS, 1, tb), F32),
                        pltpu.VMEM((eb, tb), BF16), pltpu.VMEM((d, tb), F32)],
        compiler_params=_cparams(("parallel", "parallel", "arbitrary")),
        name="peer",
    )(x, shift, scale, wqt, k1, k2, u, vt)


def _rope_tables(length, dim, ctx_len):
    rows = length // GRID_W
    n_freq = dim // 4
    inv = ROPE_BASE ** (-jnp.arange(n_freq, dtype=F32) / n_freq)
    r = jnp.repeat(jnp.arange(rows, dtype=F32), GRID_W)
    c = jnp.tile(jnp.arange(GRID_W, dtype=F32), rows)
    ang = jnp.concatenate([r[:, None] * inv, c[:, None] * inv], axis=-1)
    cos = jnp.concatenate([jnp.ones((ctx_len, dim // 2), F32), jnp.cos(ang)], axis=0)
    sin = jnp.concatenate([jnp.zeros((ctx_len, dim // 2), F32), jnp.sin(ang)], axis=0)
    return cos, sin


def _apply_rope(x, cos, sin):
    x1, x2 = jnp.split(x, 2, axis=-1)
    return jnp.concatenate([x1 * cos - x2 * sin, x2 * cos + x1 * sin], axis=-1)


def _flip(t):
    return jnp.flip(t, axis=2)


def _retention_dir(q, k, v, log_gamma, state0, strict):
    b, h, L, dk = q.shape
    dv = v.shape[-1]
    C = RET_CHUNK
    n = L // C
    qc = q.reshape(b, h, n, C, dk)
    kc = k.reshape(b, h, n, C, dk)
    vc = v.reshape(b, h, n, C, dv)
    pos = jnp.arange(C, dtype=F32)
    dist = pos[:, None] - pos[None, :]
    mask = (dist > 0) if strict else (dist >= 0)
    dmat = jnp.where(mask, jnp.exp(jnp.where(mask, dist, 0.0) * log_gamma[:, None, None]), 0.0)
    scores = jnp.einsum('bhncd,bhnsd->bhncs', qc, kc) * dmat[:, None]
    intra = jnp.einsum('bhncs,bhnsv->bhncv', scores, vc)
    q_dec = jnp.exp((pos + 1.0) * log_gamma[:, None])
    k_dec = jnp.exp((C - 1.0 - pos) * log_gamma[:, None])
    chunk_dec = jnp.exp(C * log_gamma)[None, :, None, None]
    kv = jnp.einsum('bhncd,hc,bhncv->nbhdv', kc, k_dec, vc)

    def step(s, kv_n):
        return s * chunk_dec + kv_n, s
    s_final, s_start = lax.scan(step, state0, kv)
    inter = jnp.einsum('bhncd,hc,nbhdv->bhncv', qc, q_dec, s_start)
    return (intra + inter).reshape(b, h, L, dv), s_final


def _group_norm_heads(o, g, eps=1e-5):
    oc = o - jnp.mean(o, -1, keepdims=True)
    y = oc * lax.rsqrt(jnp.mean(oc * oc, -1, keepdims=True) + eps)
    b, h, L, dv = o.shape
    return jnp.transpose(y, (0, 2, 1, 3)).reshape(b, L, h * dv) * g


def _retention(rq, rk, rv, cos, sin, lc):
    b, t, _ = rq.shape
    q = _apply_rope(rq.reshape(b, t, RET_HEADS, RET_DK), cos[:, None], sin[:, None])
    k = _apply_rope(rk.reshape(b, t, RET_HEADS, RET_DK), cos[:, None], sin[:, None]) * RET_DK ** -0.5
    v = rv.reshape(b, t, RET_HEADS, RET_DV)
    tr = lambda a: jnp.transpose(a, (0, 2, 1, 3))
    q, k, v = tr(q), tr(k), tr(v)
    log_gamma = jnp.log1p(-jnp.exp2(-5.0 - jnp.arange(RET_HEADS, dtype=F32)))
    zero = jnp.zeros((b, RET_HEADS, RET_DK, RET_DV), F32)
    qc, kc, vc = q[:, :, :lc], k[:, :, :lc], v[:, :, :lc]
    ql, kl, vl = q[:, :, lc:], k[:, :, lc:], v[:, :, lc:]
    oc_f, s_f = _retention_dir(qc, kc, vc, log_gamma, zero, False)
    oc_b, s_b = _retention_dir(_flip(qc), _flip(kc), _flip(vc), log_gamma, zero, True)
    ol_f, _ = _retention_dir(ql, kl, vl, log_gamma, s_f, False)
    ol_b, _ = _retention_dir(_flip(ql), _flip(kl), _flip(vl), log_gamma, s_b, True)
    return jnp.concatenate([oc_f + _flip(oc_b), ol_f + _flip(ol_b)], axis=2)


def _gated_delta_dir(q, k, v, g, beta, state0):
    b, h, L, dk = q.shape
    dv = v.shape[-1]
    C = DN_CHUNK
    n = L // C
    qc = q.reshape(b, h, n, C, dk)
    kc = k.reshape(b, h, n, C, dk)
    vc = v.reshape(b, h, n, C, dv)
    gc = jnp.cumsum(g.reshape(b, h, n, C), axis=-1)
    bc = beta.reshape(b, h, n, C, 1)
    pos = jnp.arange(C)
    incl = pos[:, None] >= pos[None, :]
    strict = pos[:, None] > pos[None, :]
    gdiff = gc[..., :, None] - gc[..., None, :]
    decay = jnp.where(incl, jnp.exp(jnp.where(incl, gdiff, 0.0)), 0.0)
    kb = kc * bc
    lmat = jnp.where(strict, jnp.einsum('bhncd,bhnsd->bhncs', kb, kc) * decay, 0.0)
    eye = jnp.eye(C, dtype=F32)
    t_inv = lax.linalg.triangular_solve(eye + lmat, jnp.broadcast_to(eye, lmat.shape), left_side=True,
                                        lower=True, unit_diagonal=True)
    u = jnp.einsum('bhncs,bhnsv->bhncv', t_inv, vc * bc)
    w = jnp.einsum('bhncs,bhnsd->bhncd', t_inv, kb * jnp.exp(gc)[..., None])
    attn = jnp.einsum('bhncd,bhnsd->bhncs', qc, kc) * decay
    q_dec = qc * jnp.exp(gc)[..., None]
    k_dec = kc * jnp.exp(gc[..., -1:] - gc)[..., None]
    chunk_dec = jnp.exp(gc[..., -1])
    xs = tuple(jnp.moveaxis(a, 2, 0) for a in (attn, q_dec, k_dec, u, w, chunk_dec))

    def step(s, inp):
        a_n, qd_n, kd_n, u_n, w_n, cd_n = inp
        v_new = u_n - jnp.einsum('bhcd,bhdv->bhcv', w_n, s)
        o_n = jnp.einsum('bhcd,bhdv->bhcv', qd_n, s) + jnp.einsum('bhcs,bhsv->bhcv', a_n, v_new)
        s = s * cd_n[..., None, None] + jnp.einsum('bhcd,bhcv->bhdv', kd_n, v_new)
        return s, o_n
    s_final, o = lax.scan(step, state0, xs)
    return jnp.moveaxis(o, 0, 2).reshape(b, h, L, dv), s_final


def _short_conv(x, w):
    K = w.shape[0]
    return lax.conv_general_dilated(x, w[:, None, :], window_strides=(1,),
                                    padding=[((K - 1) // 2, K // 2)],
                                    dimension_numbers=('NWC', 'WIO', 'NWC'),
                                    feature_group_count=x.shape[-1])


def _l2_norm(x, eps=1e-6):
    return x * lax.rsqrt(jnp.sum(x * x, -1, keepdims=True) + eps)


def _dn_heads(qkv, ab, conv_w, a_log, dt_bias):
    b, L, _ = qkv.shape
    qkv = jax.nn.silu(_short_conv(qkv, conv_w))
    hk = DN_HEADS * DN_DK
    q, k, v = qkv[..., :hk], qkv[..., hk:2 * hk], qkv[..., 2 * hk:]
    q = _l2_norm(q.reshape(b, L, DN_HEADS, DN_DK)) * DN_DK ** -0.5
    k = _l2_norm(k.reshape(b, L, DN_HEADS, DN_DK))
    v = v.reshape(b, L, DN_HEADS, DN_DV)
    a = ab[..., :2 * DN_HEADS].reshape(b, L, 2, DN_HEADS)
    bt = ab[..., 2 * DN_HEADS:4 * DN_HEADS].reshape(b, L, 2, DN_HEADS)
    g = -jnp.exp(a_log) * jax.nn.softplus(a + dt_bias)
    beta = jax.nn.sigmoid(bt)
    tr = lambda t: jnp.transpose(t, (0, 2, 1, 3))
    return tr(q), tr(k), tr(v), jnp.transpose(g, (2, 0, 3, 1)), jnp.transpose(beta, (2, 0, 3, 1))


def _deltanet(qkv, ab, gate, conv_w, a_log, dt_bias, norm_g, lc):
    b, t, _ = qkv.shape
    ql, kl, vl, gl, bl = _dn_heads(qkv[:, lc:], ab[:, lc:], conv_w, a_log, dt_bias)
    qc, kc, vc, gc, bc = _dn_heads(qkv[:, :lc], ab[:, :lc], conv_w, a_log, dt_bias)
    zero = jnp.zeros((b, DN_HEADS, DN_DK, DN_DV), F32)
    oc_f, s_f = _gated_delta_dir(qc, kc, vc, gc[0], bc[0], zero)
    oc_b, s_b = _gated_delta_dir(_flip(qc), _flip(kc), _flip(vc), _flip(gc[1]), _flip(bc[1]), zero)
    ol_f, _ = _gated_delta_dir(ql, kl, vl, gl[0], bl[0], s_f)
    ol_b, _ = _gated_delta_dir(_flip(ql), _flip(kl), _flip(vl), _flip(gl[1]), _flip(bl[1]), s_b)
    o = jnp.concatenate([oc_f + _flip(oc_b), ol_f + _flip(ol_b)], axis=2)
    o = jnp.transpose(o, (0, 2, 1, 3))
    y = o * lax.rsqrt(jnp.mean(o * o, -1, keepdims=True) + 1e-6) * norm_g
    y = y * jax.nn.silu(gate.reshape(b, t, DN_HEADS, DN_DV))
    return y.reshape(b, t, DN_HEADS * DN_DV)


def _pad_cols(w, width):
    return jnp.pad(w, ((0, 0), (0, width - w.shape[1])))


def kernel(x, c, ctx, c_ctx, ada_w, ada_b, ln1_g, ln1_b, ln2_g, ln2_b, ar_w_in, mla_q_norm, mla_w_uq,
           mla_kv_norm, mla_w_ukv, ret_gn_g, ar_w_out, dn_w_in, dn_conv, dn_a_log, dn_dt_bias, dn_norm_g,
           dn_w_out, peer_w_q, peer_k1, peer_k2, peer_u, peer_v):
    bsz, L, D = x.shape
    lc = ctx.shape[1]
    T = lc + L
    tm = 256
    assert lc % tm == 0 and L % tm == 0
    cb = lc // tm

    X = jnp.concatenate([ctx, x], axis=1)
    cc = jnp.zeros((8, D), F32).at[:bsz].set(c).at[bsz].set(c_ctx)
    mod_all = ada_all(cc, ada_w, ada_b)

    cos_m, sin_m = _rope_tables(L, MLA_ROPE, lc)
    cos_r, sin_r = _rope_tables(L, RET_DK, lc)

    for l in range(DEPTH):
        j = l // 2
        mod = mod_all[l]
        ml = mod[:bsz].reshape(bsz, N_MOD, D)
        mc = jnp.broadcast_to(mod[bsz].reshape(1, N_MOD, D), (bsz, N_MOD, D))
        msel = jnp.stack([mc, ml], axis=1)
        mvec = [msel[:, :, i][:, :, None, :] for i in range(N_MOD)]
        sh1, sc1, g1, sh2, sc2, g2 = mvec

        if l % 2 == 0:
            w_in = ar_w_in[j]
            w_pad = jnp.concatenate(
                [w_in[:, :416], jnp.zeros((D, 96), F32), w_in[:, 416:]], axis=1).astype(BF16)
            splits = [(0, 256), (256, 384), (384, 512), (512, 1024), (1024, 1536), (1536, 2048), (2048, 2560)]
            cq, ckv, krp, rq, rk, rv, rg = proj(X, w_pad, splits, mode="mod", shift=sh1, scale=sc1,
                                                ctx_blocks=cb, tm=tm, name="ar_in")
            (qf,) = proj(cq, mla_w_uq[j].astype(BF16), [(0, MLA_HEADS * (MLA_NOPE + MLA_ROPE))], mode="rms",
                         gain=mla_q_norm[j], tm=tm, name="mla_uq")
            (kvf,) = proj(ckv, mla_w_ukv[j].astype(BF16), [(0, MLA_HEADS * (MLA_NOPE + MLA_V))], mode="rms",
                          gain=mla_kv_norm[j], tm=tm, name="mla_ukv")
            qf = qf.reshape(bsz, T, MLA_HEADS, MLA_NOPE + MLA_ROPE)
            kvf = kvf.reshape(bsz, T, MLA_HEADS, MLA_NOPE + MLA_V)
            qn, qr = qf[..., :MLA_NOPE], qf[..., MLA_NOPE:]
            kn, vv = kvf[..., :MLA_NOPE], kvf[..., MLA_NOPE:]
            qr = _apply_rope(qr, cos_m[:, None], sin_m[:, None])
            kr = _apply_rope(krp[..., :MLA_ROPE], cos_m, sin_m)
            qh = jnp.transpose(jnp.concatenate([qn, qr], -1) * MLA_SCALE, (0, 2, 1, 3)).astype(BF16)
            kh = jnp.transpose(jnp.concatenate(
                [kn, jnp.broadcast_to(kr[:, :, None, :], (bsz, T, MLA_HEADS, MLA_ROPE))], -1), (0, 2, 1, 3)).astype(BF16)
            vh = jnp.transpose(vv, (0, 2, 1, 3)).astype(BF16)
            o_l = attention(qh[:, :, lc:], kh, vh, tq=512, tk=768)
            o_c = attention(qh[:, :, :lc], kh[:, :, :lc], vh[:, :, :lc], tq=lc, tk=lc)
            mla = jnp.transpose(jnp.concatenate([o_c, o_l], axis=2), (0, 2, 1, 3)).reshape(bsz, T, MLA_HEADS * MLA_V)
            ro = _retention(rq, rk, rv, cos_r, sin_r, lc)
            ret = _group_norm_heads(ro, ret_gn_g[j]) * jax.nn.silu(rg)
            mix = jnp.concatenate([mla, ret], axis=-1)
            (y,) = proj(mix, ar_w_out[j].astype(BF16), [(0, D)], tm=tm, name="ar_out")
        else:
            w_pad = _pad_cols(dn_w_in[j], 4224).astype(BF16)
            splits = [(0, 3072), (3072, 4096), (4096, 4224)]
            qkv, gate, ab = proj(X, w_pad, splits, mode="mod", shift=sh1, scale=sc1, ctx_blocks=cb, tm=tm,
                                 name="dn_in")
            mix = _deltanet(qkv, ab, gate, dn_conv[j], dn_a_log[j], dn_dt_bias[j], dn_norm_g[j], lc)
            (y,) = proj(mix, dn_w_out[j].astype(BF16), [(0, D)], tm=tm, name="dn_out")

        X = resid_ln(X, y, g1, ln1_g[l], ln1_b[l], ctx_blocks=cb, tm=tm)
        f = peer(X, sh2, sc2, peer_w_q[l].T.astype(BF16), peer_k1[l].astype(BF16), peer_k2[l].astype(BF16),
                 peer_u[l].astype(BF16), peer_v[l].T.astype(BF16), ctx_blocks=cb)
        X = resid_ln(X, f, g2, ln2_g[l], ln2_b[l], ctx_blocks=cb, tm=tm)

    return X[:, lc:]
```

```python
import functools
import math

import numpy as np
import jax
import jax.numpy as jnp
from jax import lax
from jax.experimental import pallas as pl
from jax.experimental.pallas import tpu as pltpu

F32 = jnp.float32
BF16 = jnp.bfloat16

DEPTH = 4
GRID_W = 64
ROPE_BASE = 10000.0
N_MOD = 6

MLA_HEADS = 8
MLA_Q_RANK = 256
MLA_KV_RANK = 128
MLA_NOPE = 64
MLA_ROPE = 32
MLA_V = 64
MLA_SCALE = (MLA_NOPE + MLA_ROPE) ** -0.5

RET_HEADS = 8
RET_DK = 64
RET_DV = 64
RET_CHUNK = 128

DN_HEADS = 8
DN_DK = 128
DN_DV = 128
DN_CONV = 5
DN_CHUNK = 64

PEER_HEADS = 8
PEER_KEYS = 128
PEER_QDIM = 256
PEER_TOPK = 16

DEEPNORM_ALPHA = (2 * DEPTH) ** 0.25

LANE = 128
VMEM_LIMIT = 56 * 1024 * 1024

NT = (((1,), (1,)), ((), ()))
TN = (((0,), (0,)), ((), ()))


def _cparams(sem):
    return pltpu.CompilerParams(dimension_semantics=sem, vmem_limit_bytes=VMEM_LIMIT)


def _ada_kernel(c_ref, w_ref, b_ref, o_ref):
    c = c_ref[...]
    a = (c * jax.nn.sigmoid(c)).astype(BF16)
    o_ref[...] = jnp.dot(a, w_ref[...].astype(BF16), preferred_element_type=F32) + b_ref[...]


def ada_all(cc, ada_w, ada_b, tn=1024):
    depth, d, n = ada_w.shape
    m = cc.shape[0]
    return pl.pallas_call(
        _ada_kernel,
        grid=(depth, n // tn),
        in_specs=[
            pl.BlockSpec((m, d), lambda l, j: (0, 0)),
            pl.BlockSpec((None, d, tn), lambda l, j: (l, 0, j)),
            pl.BlockSpec((None, 1, tn), lambda l, j: (l, 0, j)),
        ],
        out_specs=pl.BlockSpec((None, m, tn), lambda l, j: (l, 0, j)),
        out_shape=jax.ShapeDtypeStruct((depth, m, n), F32),
        compiler_params=_cparams(("arbitrary", "arbitrary")),
        name="ada",
    )(cc, ada_w, ada_b.reshape(depth, 1, n))


def _proj_kernel(*refs, mode, splits, eps):
    if mode == "mod":
        x_ref, sh_ref, sc_ref, w_ref = refs[:4]
        outs = refs[4:]
        x = x_ref[...] * (1.0 + sc_ref[...]) + sh_ref[...]
    elif mode == "rms":
        x_ref, g_ref, w_ref = refs[:3]
        outs = refs[3:]
        x = x_ref[...]
        x = x * lax.rsqrt(jnp.mean(x * x, axis=-1, keepdims=True) + eps) * g_ref[...]
    else:
        x_ref, w_ref = refs[:2]
        outs = refs[2:]
        x = x_ref[...]
    z = jnp.dot(x.astype(BF16), w_ref[...], preferred_element_type=F32)
    for o_ref, (a, b) in zip(outs, splits):
        o_ref[...] = z[:, a:b].astype(o_ref.dtype)


def proj(x, w, splits, *, mode="none", shift=None, scale=None, gain=None, ctx_blocks=1, tm=256,
         eps=1e-6, name="proj"):
    bsz, t, k = x.shape
    n = w.shape[1]
    row = lambda b, j: (b, j, 0)
    sel = lambda b, j: (b, jnp.minimum(j // ctx_blocks, 1), 0, 0)
    in_specs = [pl.BlockSpec((None, tm, k), row)]
    args = [x]
    if mode == "mod":
        in_specs += [pl.BlockSpec((None, None, 1, k), sel)] * 2
        args += [shift, scale]
    elif mode == "rms":
        in_specs += [pl.BlockSpec((1, k), lambda b, j: (0, 0))]
        args += [gain.reshape(1, k)]
    in_specs += [pl.BlockSpec((k, n), lambda b, j: (0, 0))]
    args += [w]
    return pl.pallas_call(
        functools.partial(_proj_kernel, mode=mode, splits=tuple(splits), eps=eps),
        grid=(bsz, t // tm),
        in_specs=in_specs,
        out_specs=[pl.BlockSpec((None, tm, b - a), row) for a, b in splits],
        out_shape=[jax.ShapeDtypeStruct((bsz, t, b - a), F32) for a, b in splits],
        compiler_params=_cparams(("parallel", "parallel")),
        name=name,
    )(*args)


def _resid_ln_kernel(x_ref, y_ref, gate_ref, g_ref, b_ref, o_ref, *, eps):
    v = DEEPNORM_ALPHA * x_ref[...] + gate_ref[...] * y_ref[...]
    vc = v - jnp.mean(v, axis=-1, keepdims=True)
    var = jnp.mean(vc * vc, axis=-1, keepdims=True)
    o_ref[...] = vc * lax.rsqrt(var + eps) * g_ref[...] + b_ref[...]


def resid_ln(x, y, gate, g, b, *, ctx_blocks=1, tm=256, eps=1e-5):
    bsz, t, d = x.shape
    row = lambda b_, j: (b_, j, 0)
    sel = lambda b_, j: (b_, jnp.minimum(j // ctx_blocks, 1), 0, 0)
    vec = lambda b_, j: (0, 0)
    return pl.pallas_call(
        functools.partial(_resid_ln_kernel, eps=eps),
        grid=(bsz, t // tm),
        in_specs=[pl.BlockSpec((None, tm, d), row), pl.BlockSpec((None, tm, d), row),
                  pl.BlockSpec((None, None, 1, d), sel),
                  pl.BlockSpec((1, d), vec), pl.BlockSpec((1, d), vec)],
        out_specs=pl.BlockSpec((None, tm, d), row),
        out_shape=jax.ShapeDtypeStruct((bsz, t, d), F32),
        compiler_params=_cparams(("parallel", "parallel")),
        name="resid_ln",
    )(x, y, gate, g.reshape(1, d), b.reshape(1, d))


def _attn_kernel(q_ref, k_ref, v_ref, o_ref, *, tk, nk):
    q = q_ref[...]
    tq = q.shape[0]
    dv = v_ref.shape[-1]

    def body(i, carry):
        m, l, acc = carry
        start = pl.multiple_of(i * tk, tk)
        k = k_ref[pl.ds(start, tk), :]
        v = v_ref[pl.ds(start, tk), :]
        s = lax.dot_general(q, k, NT, preferred_element_type=F32)
        m_new = jnp.maximum(m, jnp.max(s, axis=1, keepdims=True))
        p = jnp.exp(s - m_new)
        alpha = jnp.exp(m - m_new)
        l = alpha * l + jnp.sum(p, axis=1, keepdims=True)
        acc = alpha * acc + jnp.dot(p.astype(BF16), v, preferred_element_type=F32)
        return m_new, l, acc

    m0 = jnp.full((tq, 1), -jnp.inf, F32)
    l0 = jnp.zeros((tq, 1), F32)
    a0 = jnp.zeros((tq, dv), F32)
    _, l, acc = lax.fori_loop(0, nk, body, (m0, l0, a0))
    o_ref[...] = acc / l


def attention(q, k, v, *, tq, tk):
    bsz, h, lq, dq = q.shape
    lk, dv = v.shape[2], v.shape[3]
    assert lq % tq == 0 and lk % tk == 0
    return pl.pallas_call(
        functools.partial(_attn_kernel, tk=tk, nk=lk // tk),
        grid=(bsz, h, lq // tq),
        in_specs=[pl.BlockSpec((None, None, tq, dq), lambda b, hh, i: (b, hh, i, 0)),
                  pl.BlockSpec((None, None, lk, dq), lambda b, hh, i: (b, hh, 0, 0)),
                  pl.BlockSpec((None, None, lk, dv), lambda b, hh, i: (b, hh, 0, 0))],
        out_specs=pl.BlockSpec((None, None, tq, dv), lambda b, hh, i: (b, hh, i, 0)),
        out_shape=jax.ShapeDtypeStruct((bsz, h, lq, dv), F32),
        compiler_params=_cparams(("parallel", "parallel", "parallel")),
        name="mla_attn",
    )(q, k, v)


def _top_values(s, k):
    vals = []
    cur = s
    for i in range(k):
        m = jnp.max(cur, axis=0, keepdims=True)
        vals.append(m)
        if i + 1 < k:
            cur = jnp.where(cur >= m, -jnp.inf, cur)
    return vals


def _gelu_exact(x):
    return 0.5 * x * (1.0 + lax.erf(x * (2.0 ** -0.5)))


def _peer_kernel(x_ref, sh_ref, sc_ref, wqt_ref, k1_ref, k2_ref, u_ref, vt_ref, o_ref,
                 xb_ref, s1_ref, s2_ref, e1_ref, e2_ref, tau_ref, a_ref, acc_ref, *, rows):
    e = pl.program_id(2)
    ne = pl.num_programs(2)
    half = PEER_QDIM // 2

    @pl.when(e == 0)
    def _prepare():
        x = x_ref[...] * (1.0 + sc_ref[...]) + sh_ref[...]
        xb = x.astype(BF16)
        xb_ref[...] = xb
        qt = lax.dot_general(wqt_ref[...], xb, NT, preferred_element_type=F32)
        for h in range(PEER_HEADS):
            q1 = qt[h * PEER_QDIM: h * PEER_QDIM + half].astype(BF16)
            q2 = qt[h * PEER_QDIM + half: (h + 1) * PEER_QDIM].astype(BF16)
            s1 = jnp.dot(k1_ref[h], q1, preferred_element_type=F32)
            s2 = jnp.dot(k2_ref[h], q2, preferred_element_type=F32)
            v1 = _top_values(s1, PEER_TOPK)
            v2 = _top_values(s2, PEER_TOPK)
            cands = [v1[a] + v2[b] for a in range(PEER_TOPK) for b in range(PEER_TOPK)
                     if (a + 1) * (b + 1) <= PEER_TOPK]
            top = cands[0]
            cur = list(cands)
            for i in range(PEER_TOPK):
                m = functools.reduce(jnp.maximum, cur)
                if i + 1 < PEER_TOPK:
                    cur = [jnp.where(c >= m, -jnp.inf, c) for c in cur]
            tau = m
            z = functools.reduce(
                jnp.add, [jnp.where(c >= tau, jnp.exp(c - top), 0.0) for c in cands])
            s1_ref[h] = s1
            s2_ref[h] = s2
            e1_ref[h] = jnp.exp(s1 - v1[0]) / z
            e2_ref[h] = jnp.exp(s2 - v2[0])
            tau_ref[h] = tau
        acc_ref[...] = jnp.zeros_like(acc_ref)

    ht = lax.dot_general(u_ref[...], xb_ref[...], NT, preferred_element_type=F32)
    r0 = e * rows
    tb = ht.shape[1]
    for r in range(rows):
        def head(h, w, r=r):
            s1row = s1_ref[h, pl.ds(r0 + r, 1), :]
            e1row = e1_ref[h, pl.ds(r0 + r, 1), :]
            sums = s1row + s2_ref[h]
            return w + jnp.where(sums >= tau_ref[h], e1row * e2_ref[h], 0.0)
        w = lax.fori_loop(0, PEER_HEADS, head, jnp.zeros((PEER_KEYS, tb), F32))
        hr = ht[r * PEER_KEYS:(r + 1) * PEER_KEYS]
        a_ref[r * PEER_KEYS:(r + 1) * PEER_KEYS, :] = (w * _gelu_exact(hr)).astype(BF16)
    acc_ref[...] += jnp.dot(vt_ref[...], a_ref[...], preferred_element_type=F32)

    @pl.when(e == ne - 1)
    def _finish():
        o_ref[...] = acc_ref[...].T


def peer(x, shift, scale, wqt, k1, k2, u, vt, *, ctx_blocks, tb=256, eb=1024):
    bsz, t, d = x.shape
    n = u.shape[0]
    rows = eb // PEER_KEYS
    row = lambda b, j, e: (b, j, 0)
    sel = lambda b, j, e: (b, jnp.minimum(j // ctx_blocks, 1), 0, 0)
    const2 = lambda b, j, e: (0, 0)
    const3 = lambda b, j, e: (0, 0, 0)
    tab = pltpu.VMEM((PEER_HEADS, PEER_KEYS, tb), F32)
    return pl.pallas_call(
        functools.partial(_peer_kernel, rows=rows),
        grid=(bsz, t // tb, n // eb),
        in_specs=[pl.BlockSpec((None, tb, d), row),
                  pl.BlockSpec((None, None, 1, d), sel), pl.BlockSpec((None, None, 1, d), sel),
                  pl.BlockSpec(wqt.shape, const2),
                  pl.BlockSpec(k1.shape, const3), pl.BlockSpec(k2.shape, const3),
                  pl.BlockSpec((eb, d), lambda b, j, e: (e, 0)),
                  pl.BlockSpec((d, eb), lambda b, j, e: (0, e))],
        out_specs=pl.BlockSpec((None, tb, d), row),
        out_shape=jax.ShapeDtypeStruct((bsz, t, d), F32),
        scratch_shapes=[pltpu.VMEM((tb, d), BF16), tab, tab, tab, tab,
                        pltpu.VMEM((PEER_HEADS, 1, tb), F32),
                        pltpu.VMEM((eb, tb), BF16), pltpu.VMEM((d, tb), F32)],
        compiler_params=_cparams(("parallel", "parallel", "arbitrary")),
        name="peer",
    )(x, shift, scale, wqt, k1, k2, u, vt)


def _ret_kernel(q_ref, k_ref, v_ref, d_ref, qdec_ref, kdec_ref, cdec_ref, o_ref, s_ref, *, chunk, nchunk):
    @pl.when(pl.program_id(2) == 0)
    def _init():
        s_ref[...] = jnp.zeros_like(s_ref)

    qdec = qdec_ref[...]
    kdec = kdec_ref[...]
    cdec = cdec_ref[...]

    def step(c, carry):
        new = []
        for d in range(2):
            s = carry[d]
            sl = pl.ds(pl.multiple_of(c * chunk, chunk), chunk)
            q = q_ref[d, sl, :]
            k = k_ref[d, sl, :]
            v = v_ref[d, sl, :]
            scores = lax.dot_general(q, k, NT, preferred_element_type=F32) * d_ref[d]
            intra = jnp.dot(scores.astype(BF16), v, preferred_element_type=F32)
            qd = (q.astype(F32) * qdec).astype(BF16)
            inter = jnp.dot(qd, s.astype(BF16), preferred_element_type=F32)
            o_ref[d, sl, :] = intra + inter
            kd = (k.astype(F32) * kdec).astype(BF16)
            new.append(s * cdec + lax.dot_general(kd, v, TN, preferred_element_type=F32))
        return tuple(new)

    s0, s1 = lax.fori_loop(0, nchunk, step, (s_ref[0], s_ref[1]))
    s_ref[0] = s0
    s_ref[1] = s1


def retention_scan(q, k, v, dmat, qdec, kdec, cdec, *, chunk, seg):
    _, bsz, h, t, dk = q.shape
    dv = v.shape[-1]
    assert t % seg == 0 and seg % chunk == 0
    blk = lambda d_: pl.BlockSpec((2, None, None, seg, d_), lambda b, hh, s: (0, b, hh, s, 0))
    return pl.pallas_call(
        functools.partial(_ret_kernel, chunk=chunk, nchunk=seg // chunk),
        grid=(bsz, h, t // seg),
        in_specs=[blk(dk), blk(dk), blk(dv),
                  pl.BlockSpec((2, None, chunk, chunk), lambda b, hh, s: (0, hh, 0, 0)),
                  pl.BlockSpec((None, chunk, dk), lambda b, hh, s: (hh, 0, 0)),
                  pl.BlockSpec((None, chunk, dk), lambda b, hh, s: (hh, 0, 0)),
                  pl.BlockSpec((None, dk, dv), lambda b, hh, s: (hh, 0, 0))],
        out_specs=blk(dv),
        out_shape=jax.ShapeDtypeStruct((2, bsz, h, t, dv), F32),
        scratch_shapes=[pltpu.VMEM((2, dk, dv), F32)],
        compiler_params=_cparams(("parallel", "parallel", "arbitrary")),
        name="retention",
    )(q, k, v, dmat, qdec, kdec, cdec)


def _dn_kernel(q_ref, k_ref, v_ref, aux_ref, rows_ref, o_ref,
               s_ref, m_s, b_s, qp_s, cd_s, sall_s, *, chunk, nchunk):
    C = chunk

    @pl.when(pl.program_id(2) == 0)
    def _init():
        s_ref[...] = jnp.zeros_like(s_ref)

    ri = lax.broadcasted_iota(jnp.int32, (C, C), 0)
    ci = lax.broadcasted_iota(jnp.int32, (C, C), 1)
    incl = ri >= ci
    strict = ri > ci
    eye = (ri == ci).astype(F32)

    bmm = lambda a, b: jnp.einsum('nij,njk->nik', a.astype(BF16), b.astype(BF16), preferred_element_type=F32)
    bmm_tn = lambda a, b: jnp.einsum('nci,ncj->nij', a.astype(BF16), b.astype(BF16),
                                     preferred_element_type=F32)

    for d in range(2):
        q = q_ref[d].reshape(nchunk, C, q_ref.shape[-1])
        k = k_ref[d].reshape(nchunk, C, k_ref.shape[-1])
        v = v_ref[d].reshape(nchunk, C, v_ref.shape[-1])
        aux = aux_ref[d].reshape(nchunk, C, aux_ref.shape[-1])
        gcol = aux[:, :, 0:1]
        bcol = aux[:, :, 1:2]
        grow = rows_ref[d, :, 0:1, :]
        brow = rows_ref[d, :, 1:2, :]
        kk = jnp.einsum('nid,njd->nij', k, k, preferred_element_type=F32)
        qk = jnp.einsum('nid,njd->nij', q, k, preferred_element_type=F32)
        decay = jnp.where(incl, jnp.exp(jnp.where(incl, gcol - grow, 0.0)), 0.0)
        x = jnp.where(strict, -(kk * decay * bcol), 0.0)
        x2 = bmm(x, x)
        x4 = bmm(x2, x2)
        x8 = bmm(x4, x4)
        x16 = bmm(x8, x8)
        x32 = bmm(x16, x16)
        p1 = eye + x + x2 + bmm(x, x2)
        p2 = eye + x4 + x8 + bmm(x4, x8)
        p3 = eye + x16 + x32 + bmm(x16, x32)
        tinv = bmm(bmm(p1, p2), p3)
        egrow = jnp.exp(grow)
        u = bmm(tinv * brow, v)
        w = bmm(tinv * (brow * egrow), k)
        attn = qk * decay
        gtot = grow[:, :, C - 1:C]
        qd = q.astype(F32) * jnp.exp(gcol)
        kd = k.astype(F32) * jnp.exp(gtot - gcol)
        m_s[d] = bmm_tn(kd, w).astype(BF16)
        b_s[d] = bmm_tn(kd, u)
        qp_s[d] = (qd - bmm(attn, w)).astype(BF16)
        o_ref[d] = bmm(attn, u).reshape(o_ref.shape[1:])
        cd_s[d] = jnp.broadcast_to(jnp.exp(gtot), cd_s.shape[1:])

    def step(c, carry):
        new = []
        for d in range(2):
            s = carry[d]
            sb = s.astype(BF16)
            sall_s[d, c] = sb
            new.append(s * cd_s[d, c] - jnp.dot(m_s[d, c], sb, preferred_element_type=F32) + b_s[d, c])
        return tuple(new)

    s0, s1 = lax.fori_loop(0, nchunk, step, (s_ref[0], s_ref[1]))
    s_ref[0] = s0
    s_ref[1] = s1

    for d in range(2):
        o_ref[d] += bmm(qp_s[d], sall_s[d]).reshape(o_ref.shape[1:])


def deltanet_scan(q, k, v, aux, rows, *, chunk, seg):
    _, bsz, h, t, dk = q.shape
    dv = v.shape[-1]
    assert t % seg == 0 and seg % chunk == 0
    nchunk = seg // chunk
    blk = lambda d_: pl.BlockSpec((2, None, None, seg, d_), lambda b, hh, s: (0, b, hh, s, 0))
    return pl.pallas_call(
        functools.partial(_dn_kernel, chunk=chunk, nchunk=nchunk),
        grid=(bsz, h, t // seg),
        in_specs=[blk(dk), blk(dk), blk(dv), blk(aux.shape[-1]),
                  pl.BlockSpec((2, None, None, nchunk, 8, chunk), lambda b, hh, s: (0, b, hh, s, 0, 0))],
        out_specs=blk(dv),
        out_shape=jax.ShapeDtypeStruct((2, bsz, h, t, dv), F32),
        scratch_shapes=[pltpu.VMEM((2, dk, dv), F32),
                        pltpu.VMEM((2, nchunk, dk, dv), BF16),
                        pltpu.VMEM((2, nchunk, dk, dv), F32),
                        pltpu.VMEM((2, nchunk, chunk, dk), BF16),
                        pltpu.VMEM((2, nchunk, 1, dv), F32),
                        pltpu.VMEM((2, nchunk, dk, dv), BF16)],
        compiler_params=_cparams(("parallel", "parallel", "arbitrary")),
        name="deltanet",
    )(q, k, v, aux, rows)


def _rope_tables(length, dim, ctx_len):
    rows = length // GRID_W
    n_freq = dim // 4
    inv = ROPE_BASE ** (-jnp.arange(n_freq, dtype=F32) / n_freq)
    r = jnp.repeat(jnp.arange(rows, dtype=F32), GRID_W)
    c = jnp.tile(jnp.arange(GRID_W, dtype=F32), rows)
    ang = jnp.concatenate([r[:, None] * inv, c[:, None] * inv], axis=-1)
    cos = jnp.concatenate([jnp.ones((ctx_len, dim // 2), F32), jnp.cos(ang)], axis=0)
    sin = jnp.concatenate([jnp.zeros((ctx_len, dim // 2), F32), jnp.sin(ang)], axis=0)
    return cos, sin


def _apply_rope(x, cos, sin):
    x1, x2 = jnp.split(x, 2, axis=-1)
    return jnp.concatenate([x1 * cos - x2 * sin, x2 * cos + x1 * sin], axis=-1)


def _flip_parts(a, lc, axis):
    lo = lax.slice_in_dim(a, 0, lc, axis=axis)
    hi = lax.slice_in_dim(a, lc, a.shape[axis], axis=axis)
    return jnp.concatenate([jnp.flip(lo, axis), jnp.flip(hi, axis)], axis)


def _two_dirs(a, lc, axis):
    return jnp.stack([a, _flip_parts(a, lc, axis)])


def _retention(rq, rk, rv, cos, sin, lc, seg):
    b, t, _ = rq.shape
    q = _apply_rope(rq.reshape(b, t, RET_HEADS, RET_DK), cos[:, None], sin[:, None])
    k = _apply_rope(rk.reshape(b, t, RET_HEADS, RET_DK), cos[:, None], sin[:, None]) * RET_DK ** -0.5
    v = rv.reshape(b, t, RET_HEADS, RET_DV)
    tr = lambda a: _two_dirs(jnp.transpose(a, (0, 2, 1, 3)).astype(BF16), lc, 2)
    log_gamma = jnp.log1p(-jnp.exp2(-5.0 - jnp.arange(RET_HEADS, dtype=F32)))
    C = RET_CHUNK
    pos = jnp.arange(C, dtype=F32)
    dist = pos[:, None] - pos[None, :]
    lg = log_gamma[:, None, None]
    dm = lambda mask: jnp.where(mask, jnp.exp(jnp.where(mask, dist, 0.0) * lg), 0.0)
    dmat = jnp.stack([dm(dist >= 0), dm(dist > 0)])
    qdec = jnp.broadcast_to(jnp.exp((pos + 1.0) * log_gamma[:, None])[..., None], (RET_HEADS, C, RET_DK))
    kdec = jnp.broadcast_to(jnp.exp((C - 1.0 - pos) * log_gamma[:, None])[..., None], (RET_HEADS, C, RET_DK))
    cdec = jnp.broadcast_to(jnp.exp(C * log_gamma)[:, None, None], (RET_HEADS, RET_DK, RET_DV))
    o = retention_scan(tr(q), tr(k), tr(v), dmat, qdec, kdec, cdec, chunk=C, seg=seg)
    return o[0] + _flip_parts(o[1], lc, 2)


def _group_norm_heads(o, g, eps=1e-5):
    oc = o - jnp.mean(o, -1, keepdims=True)
    y = oc * lax.rsqrt(jnp.mean(oc * oc, -1, keepdims=True) + eps)
    b, h, L, dv = o.shape
    return jnp.transpose(y, (0, 2, 1, 3)).reshape(b, L, h * dv) * g


def _short_conv(x, w):
    K = w.shape[0]
    return lax.conv_general_dilated(x, w[:, None, :], window_strides=(1,),
                                    padding=[((K - 1) // 2, K // 2)],
                                    dimension_numbers=('NWC', 'WIO', 'NWC'),
                                    feature_group_count=x.shape[-1])


def _l2_norm(x, eps=1e-6):
    return x * lax.rsqrt(jnp.sum(x * x, -1, keepdims=True) + eps)


def _dn_heads(qkv, ab, conv_w, a_log, dt_bias):
    b, L, _ = qkv.shape
    qkv = jax.nn.silu(_short_conv(qkv, conv_w))
    hk = DN_HEADS * DN_DK
    q, k, v = qkv[..., :hk], qkv[..., hk:2 * hk], qkv[..., 2 * hk:]
    q = _l2_norm(q.reshape(b, L, DN_HEADS, DN_DK)) * DN_DK ** -0.5
    k = _l2_norm(k.reshape(b, L, DN_HEADS, DN_DK))
    v = v.reshape(b, L, DN_HEADS, DN_DV)
    a = ab[..., :2 * DN_HEADS].reshape(b, L, 2, DN_HEADS)
    bt = ab[..., 2 * DN_HEADS:4 * DN_HEADS].reshape(b, L, 2, DN_HEADS)
    g = -jnp.exp(a_log) * jax.nn.softplus(a + dt_bias)
    beta = jax.nn.sigmoid(bt)
    tr = lambda t: jnp.transpose(t, (0, 2, 1, 3))
    return tr(q), tr(k), tr(v), jnp.transpose(g, (2, 0, 3, 1)), jnp.transpose(beta, (2, 0, 3, 1))


def _deltanet(qkv, ab, gate, conv_w, a_log, dt_bias, norm_g, lc, seg):
    b, t, _ = qkv.shape
    parts_l = _dn_heads(qkv[:, lc:], ab[:, lc:], conv_w, a_log, dt_bias)
    parts_c = _dn_heads(qkv[:, :lc], ab[:, :lc], conv_w, a_log, dt_bias)
    q, k, v = (jnp.concatenate([pc, pl_], axis=2) for pc, pl_ in zip(parts_c[:3], parts_l[:3]))
    g, beta = (jnp.concatenate([pc, pl_], axis=3) for pc, pl_ in zip(parts_c[3:], parts_l[3:]))
    C = DN_CHUNK
    dirs = lambda a: jnp.stack([a[0], _flip_parts(a[1], lc, 2)])
    g, beta = dirs(g), dirs(beta)
    gc = jnp.cumsum(g.reshape(2, b, DN_HEADS, t // C, C), axis=-1)
    bc = beta.reshape(2, b, DN_HEADS, t // C, C)
    rows = jnp.pad(jnp.stack([gc, bc], axis=-2), ((0, 0),) * 4 + ((0, 6), (0, 0)))
    aux = jnp.pad(jnp.stack([gc.reshape(2, b, DN_HEADS, t), beta], axis=-1),
                  ((0, 0),) * 4 + ((0, LANE - 2),))
    two = lambda a: _two_dirs(a.astype(BF16), lc, 2)
    o = deltanet_scan(two(q), two(k), two(v), aux, rows, chunk=C, seg=seg)
    o = o[0] + _flip_parts(o[1], lc, 2)
    o = jnp.transpose(o, (0, 2, 1, 3))
    y = o * lax.rsqrt(jnp.mean(o * o, -1, keepdims=True) + 1e-6) * norm_g
    y = y * jax.nn.silu(gate.reshape(b, t, DN_HEADS, DN_DV))
    return y.reshape(b, t, DN_HEADS * DN_DV)


def _pad_cols(w, width):
    return jnp.pad(w, ((0, 0), (0, width - w.shape[1])))


def kernel(x, c, ctx, c_ctx, ada_w, ada_b, ln1_g, ln1_b, ln2_g, ln2_b, ar_w_in, mla_q_norm, mla_w_uq,
           mla_kv_norm, mla_w_ukv, ret_gn_g, ar_w_out, dn_w_in, dn_conv, dn_a_log, dn_dt_bias, dn_norm_g,
           dn_w_out, peer_w_q, peer_k1, peer_k2, peer_u, peer_v):
    bsz, L, D = x.shape
    lc = ctx.shape[1]
    T = lc + L
    tm = 256
    assert lc % tm == 0 and L % tm == 0
    cb = lc // tm
    seg = 768 if T % 768 == 0 else tm

    X = jnp.concatenate([ctx, x], axis=1)
    cc = jnp.zeros((8, D), F32).at[:bsz].set(c).at[bsz].set(c_ctx)
    mod_all = ada_all(cc, ada_w, ada_b)

    cos_m, sin_m = _rope_tables(L, MLA_ROPE, lc)
    cos_r, sin_r = _rope_tables(L, RET_DK, lc)

    for l in range(DEPTH):
        j = l // 2
        mod = mod_all[l]
        ml = mod[:bsz].reshape(bsz, N_MOD, D)
        mc = jnp.broadcast_to(mod[bsz].reshape(1, N_MOD, D), (bsz, N_MOD, D))
        msel = jnp.stack([mc, ml], axis=1)
        mvec = [msel[:, :, i][:, :, None, :] for i in range(N_MOD)]
        sh1, sc1, g1, sh2, sc2, g2 = mvec

        if l % 2 == 0:
            w_in = ar_w_in[j]
            w_pad = jnp.concatenate(
                [w_in[:, :416], jnp.zeros((D, 96), F32), w_in[:, 416:]], axis=1).astype(BF16)
            splits = [(0, 256), (256, 384), (384, 512), (512, 1024), (1024, 1536), (1536, 2048), (2048, 2560)]
            cq, ckv, krp, rq, rk, rv, rg = proj(X, w_pad, splits, mode="mod", shift=sh1, scale=sc1,
                                                ctx_blocks=cb, tm=tm, name="ar_in")
            (qf,) = proj(cq, mla_w_uq[j].astype(BF16), [(0, MLA_HEADS * (MLA_NOPE + MLA_ROPE))], mode="rms",
                         gain=mla_q_norm[j], tm=tm, name="mla_uq")
            (kvf,) = proj(ckv, mla_w_ukv[j].astype(BF16), [(0, MLA_HEADS * (MLA_NOPE + MLA_V))], mode="rms",
                          gain=mla_kv_norm[j], tm=tm, name="mla_ukv")
            qf = qf.reshape(bsz, T, MLA_HEADS, MLA_NOPE + MLA_ROPE)
            kvf = kvf.reshape(bsz, T, MLA_HEADS, MLA_NOPE + MLA_V)
            qn, qr = qf[..., :MLA_NOPE], qf[..., MLA_NOPE:]
            kn, vv = kvf[..., :MLA_NOPE], kvf[..., MLA_NOPE:]
            qr = _apply_rope(qr, cos_m[:, None], sin_m[:, None])
            kr = _apply_rope(krp[..., :MLA_ROPE], cos_m, sin_m)
            qh = jnp.transpose(jnp.concatenate([qn, qr], -1) * MLA_SCALE, (0, 2, 1, 3)).astype(BF16)
            kh = jnp.transpose(jnp.concatenate(
                [kn, jnp.broadcast_to(kr[:, :, None, :], (bsz, T, MLA_HEADS, MLA_ROPE))], -1), (0, 2, 1, 3)).astype(BF16)
            vh = jnp.transpose(vv, (0, 2, 1, 3)).astype(BF16)
            o_l = attention(qh[:, :, lc:], kh, vh, tq=512, tk=768)
            o_c = attention(qh[:, :, :lc], kh[:, :, :lc], vh[:, :, :lc], tq=lc, tk=lc)
            mla = jnp.transpose(jnp.concatenate([o_c, o_l], axis=2), (0, 2, 1, 3)).reshape(bsz, T, MLA_HEADS * MLA_V)
            ro = _retention(rq, rk, rv, cos_r, sin_r, lc, seg)
            ret = _group_norm_heads(ro, ret_gn_g[j]) * jax.nn.silu(rg)
            mix = jnp.concatenate([mla, ret], axis=-1)
            (y,) = proj(mix, ar_w_out[j].astype(BF16), [(0, D)], tm=tm, name="ar_out")
        else:
            w_pad = _pad_cols(dn_w_in[j], 4224).astype(BF16)
            splits = [(0, 3072), (3072, 4096), (4096, 4224)]
            qkv, gate, ab = proj(X, w_pad, splits, mode="mod", shift=sh1, scale=sc1, ctx_blocks=cb, tm=tm,
                                 name="dn_in")
            mix = _deltanet(qkv, ab, gate, dn_conv[j], dn_a_log[j], dn_dt_bias[j], dn_norm_g[j], lc, seg)
            (y,) = proj(mix, dn_w_out[j].astype(BF16), [(0, D)], tm=tm, name="dn_out")

        X = resid_ln(X, y, g1, ln1_g[l], ln1_b[l], ctx_blocks=cb, tm=tm)
        f = peer(X, sh2, sc2, peer_w_q[l].T.astype(BF16), peer_k1[l].astype(BF16), peer_k2[l].astype(BF16),
                 peer_u[l].astype(BF16), peer_v[l].T.astype(BF16), ctx_blocks=cb)
        X = resid_ln(X, f, g2, ln2_g[l], ln2_b[l], ctx_blocks=cb, tm=tm)

    return X[:, lc:]
```

```python
import functools
import math

import numpy as np
import jax
import jax.numpy as jnp
from jax import lax
from jax.experimental import pallas as pl
from jax.experimental.pallas import tpu as pltpu

F32 = jnp.float32
BF16 = jnp.bfloat16

DEPTH = 4
GRID_W = 64
ROPE_BASE = 10000.0
N_MOD = 6

MLA_HEADS = 8
MLA_Q_RANK = 256
MLA_KV_RANK = 128
MLA_NOPE = 64
MLA_ROPE = 32
MLA_V = 64
MLA_SCALE = (MLA_NOPE + MLA_ROPE) ** -0.5

RET_HEADS = 8
RET_DK = 64
RET_DV = 64
RET_CHUNK = 128

DN_HEADS = 8
DN_DK = 128
DN_DV = 128
DN_CONV = 5
DN_CHUNK = 64

PEER_HEADS = 8
PEER_KEYS = 128
PEER_QDIM = 256
PEER_TOPK = 16

DEEPNORM_ALPHA = (2 * DEPTH) ** 0.25

LANE = 128
VMEM_LIMIT = 56 * 1024 * 1024

NT = (((1,), (1,)), ((), ()))
TN = (((0,), (0,)), ((), ()))


def _cparams(sem):
    return pltpu.CompilerParams(dimension_semantics=sem, vmem_limit_bytes=VMEM_LIMIT)


def _ada_kernel(c_ref, w_ref, b_ref, o_ref):
    c = c_ref[...]
    a = (c * jax.nn.sigmoid(c)).astype(BF16)
    o_ref[...] = jnp.dot(a, w_ref[...].astype(BF16), preferred_element_type=F32) + b_ref[...]


def ada_all(cc, ada_w, ada_b, tn=1024):
    depth, d, n = ada_w.shape
    m = cc.shape[0]
    return pl.pallas_call(
        _ada_kernel,
        grid=(depth, n // tn),
        in_specs=[
            pl.BlockSpec((m, d), lambda l, j: (0, 0)),
            pl.BlockSpec((None, d, tn), lambda l, j: (l, 0, j)),
            pl.BlockSpec((None, 1, tn), lambda l, j: (l, 0, j)),
        ],
        out_specs=pl.BlockSpec((None, m, tn), lambda l, j: (l, 0, j)),
        out_shape=jax.ShapeDtypeStruct((depth, m, n), F32),
        compiler_params=_cparams(("arbitrary", "arbitrary")),
        name="ada",
    )(cc, ada_w, ada_b.reshape(depth, 1, n))


def _proj_kernel(*refs, mode, splits, eps, group):
    if mode == "mod":
        x_ref, sh_ref, sc_ref, w_ref = refs[:4]
        outs = refs[4:]
        x = x_ref[...] * (1.0 + sc_ref[...]) + sh_ref[...]
    elif mode == "rms":
        x_ref, g_ref, w_ref = refs[:3]
        outs = refs[3:]
        x = x_ref[...]
        x = x * lax.rsqrt(jnp.mean(x * x, axis=-1, keepdims=True) + eps) * g_ref[...]
    elif mode == "gated_rms":
        a_ref, b_ref, gate_ref, g_ref, w_ref = refs[:5]
        outs = refs[5:]
        o = a_ref[...] + b_ref[...]
        gate = gate_ref[...]
        parts = []
        for h in range(o.shape[1] // group):
            oh = o[:, h * group:(h + 1) * group]
            parts.append(oh * lax.rsqrt(jnp.mean(oh * oh, axis=-1, keepdims=True) + eps))
        x = jnp.concatenate(parts, axis=1) * g_ref[...] * (gate * jax.nn.sigmoid(gate))
    else:
        x_ref, w_ref = refs[:2]
        outs = refs[2:]
        x = x_ref[...]
    z = jnp.dot(x.astype(BF16), w_ref[...], preferred_element_type=F32)
    for o_ref, (a, b) in zip(outs, splits):
        o_ref[...] = z[:, a:b].astype(o_ref.dtype)


def proj(x, w, splits, *, mode="none", shift=None, scale=None, gain=None, other=None, gate=None, group=LANE,
         ctx_blocks=1, tm=256, eps=1e-6, name="proj"):
    bsz, t, k = x.shape
    n = w.shape[1]
    row = lambda b, j: (b, j, 0)
    sel = lambda b, j: (b, jnp.minimum(j // ctx_blocks, 1), 0, 0)
    in_specs = [pl.BlockSpec((None, tm, k), row)]
    args = [x]
    if mode == "mod":
        in_specs += [pl.BlockSpec((None, None, 1, k), sel)] * 2
        args += [shift, scale]
    elif mode == "rms":
        in_specs += [pl.BlockSpec((1, k), lambda b, j: (0, 0))]
        args += [gain.reshape(1, k)]
    elif mode == "gated_rms":
        in_specs += [pl.BlockSpec((None, tm, k), row)] * 2 + [pl.BlockSpec((1, k), lambda b, j: (0, 0))]
        args += [other, gate, jnp.tile(gain, k // group).reshape(1, k)]
    in_specs += [pl.BlockSpec((k, n), lambda b, j: (0, 0))]
    args += [w]
    return pl.pallas_call(
        functools.partial(_proj_kernel, mode=mode, splits=tuple(splits), eps=eps, group=group),
        grid=(bsz, t // tm),
        in_specs=in_specs,
        out_specs=[pl.BlockSpec((None, tm, b - a), row) for a, b in splits],
        out_shape=[jax.ShapeDtypeStruct((bsz, t, b - a), F32) for a, b in splits],
        compiler_params=_cparams(("parallel", "parallel")),
        name=name,
    )(*args)


def _resid_ln_kernel(x_ref, y_ref, gate_ref, g_ref, b_ref, o_ref, *, eps):
    v = DEEPNORM_ALPHA * x_ref[...] + gate_ref[...] * y_ref[...]
    vc = v - jnp.mean(v, axis=-1, keepdims=True)
    var = jnp.mean(vc * vc, axis=-1, keepdims=True)
    o_ref[...] = vc * lax.rsqrt(var + eps) * g_ref[...] + b_ref[...]


def resid_ln(x, y, gate, g, b, *, ctx_blocks=1, tm=256, eps=1e-5):
    bsz, t, d = x.shape
    row = lambda b_, j: (b_, j, 0)
    sel = lambda b_, j: (b_, jnp.minimum(j // ctx_blocks, 1), 0, 0)
    vec = lambda b_, j: (0, 0)
    return pl.pallas_call(
        functools.partial(_resid_ln_kernel, eps=eps),
        grid=(bsz, t // tm),
        in_specs=[pl.BlockSpec((None, tm, d), row), pl.BlockSpec((None, tm, d), row),
                  pl.BlockSpec((None, None, 1, d), sel),
                  pl.BlockSpec((1, d), vec), pl.BlockSpec((1, d), vec)],
        out_specs=pl.BlockSpec((None, tm, d), row),
        out_shape=jax.ShapeDtypeStruct((bsz, t, d), F32),
        compiler_params=_cparams(("parallel", "parallel")),
        name="resid_ln",
    )(x, y, gate, g.reshape(1, d), b.reshape(1, d))


def _attn_kernel(q_ref, k_ref, v_ref, o_ref, *, tk, nk):
    q = q_ref[...]
    tq = q.shape[0]
    dv = v_ref.shape[-1]

    def body(i, carry):
        m, l, acc = carry
        start = pl.multiple_of(i * tk, tk)
        k = k_ref[pl.ds(start, tk), :]
        v = v_ref[pl.ds(start, tk), :]
        s = lax.dot_general(q, k, NT, preferred_element_type=F32)
        m_new = jnp.maximum(m, jnp.max(s, axis=1, keepdims=True))
        p = jnp.exp(s - m_new)
        alpha = jnp.exp(m - m_new)
        l = alpha * l + jnp.sum(p, axis=1, keepdims=True)
        acc = alpha * acc + jnp.dot(p.astype(BF16), v, preferred_element_type=F32)
        return m_new, l, acc

    m0 = jnp.full((tq, 1), -jnp.inf, F32)
    l0 = jnp.zeros((tq, 1), F32)
    a0 = jnp.zeros((tq, dv), F32)
    _, l, acc = lax.fori_loop(0, nk, body, (m0, l0, a0))
    o_ref[...] = acc / l


def attention(q, k, v, *, tq, tk):
    bsz, h, lq, dq = q.shape
    lk, dv = v.shape[2], v.shape[3]
    assert lq % tq == 0 and lk % tk == 0
    return pl.pallas_call(
        functools.partial(_attn_kernel, tk=tk, nk=lk // tk),
        grid=(bsz, h, lq // tq),
        in_specs=[pl.BlockSpec((None, None, tq, dq), lambda b, hh, i: (b, hh, i, 0)),
                  pl.BlockSpec((None, None, lk, dq), lambda b, hh, i: (b, hh, 0, 0)),
                  pl.BlockSpec((None, None, lk, dv), lambda b, hh, i: (b, hh, 0, 0))],
        out_specs=pl.BlockSpec((None, None, tq, dv), lambda b, hh, i: (b, hh, i, 0)),
        out_shape=jax.ShapeDtypeStruct((bsz, h, lq, dv), F32),
        compiler_params=_cparams(("parallel", "parallel", "parallel")),
        name="mla_attn",
    )(q, k, v)


def _top_values(s, k):
    vals = []
    cur = s
    for i in range(k):
        m = jnp.max(cur, axis=0, keepdims=True)
        vals.append(m)
        if i + 1 < k:
            cur = jnp.where(cur >= m, -jnp.inf, cur)
    return vals


def _gelu_exact(x):
    return 0.5 * x * (1.0 + lax.erf(x * (2.0 ** -0.5)))


def _peer_prepare(x_ref, sh_ref, sc_ref, wqt_ref, k1_ref, k2_ref, xb_ref, thr_ref, s2_ref, e1_ref, e2_ref):
    half = PEER_QDIM // 2
    K = PEER_TOPK
    H = PEER_HEADS
    x = x_ref[...] * (1.0 + sc_ref[...]) + sh_ref[...]
    xb = x.astype(BF16)
    xb_ref[...] = xb
    qt = lax.dot_general(wqt_ref[...], xb, NT, preferred_element_type=F32)
    v1s, v2s = [], []
    for h in range(H):
        q1 = qt[h * PEER_QDIM: h * PEER_QDIM + half].astype(BF16)
        q2 = qt[h * PEER_QDIM + half: (h + 1) * PEER_QDIM].astype(BF16)
        s1 = jnp.dot(k1_ref[h], q1, preferred_element_type=F32)
        s2 = jnp.dot(k2_ref[h], q2, preferred_element_type=F32)
        thr_ref[h] = s1
        s2_ref[h] = s2
        v1s.append(_top_values(s1, K + 1))
        v2s.append(_top_values(s2, K + 1))
    v1 = [jnp.concatenate([v1s[h][a] for h in range(H)], axis=0) for a in range(K + 1)]
    v2 = [jnp.concatenate([v2s[h][b] for h in range(H)], axis=0) for b in range(K + 1)]
    cands = [v1[a] + v2[b] for a in range(K + 1) for b in range(K + 1) if (a + 1) * (b + 1) <= K + 1]
    cur = list(cands)
    for i in range(K + 1):
        m = functools.reduce(jnp.maximum, cur)
        if i == K - 1:
            kth = m
        if i < K:
            cur = [jnp.where(c >= m, -jnp.inf, c) for c in cur]
    tau = 0.5 * (kth + m)
    top = cands[0]
    z = functools.reduce(jnp.add, [jnp.where(c >= tau, jnp.exp(c - top), 0.0) for c in cands])
    for h in range(H):
        s1 = thr_ref[h]
        thr_ref[h] = tau[h:h + 1] - s1
        e1_ref[h] = jnp.exp(s1 - v1s[h][0]) / z[h:h + 1]
        e2_ref[h] = jnp.exp(s2_ref[h] - v2s[h][0])


def _peer_weights(blk, rows, thr_ref, s2_ref, e1_ref, e2_ref, ht_ref, a_ref):
    tb = ht_ref.shape[1]
    r0 = blk * rows
    for r in range(rows):
        rs = slice(r * PEER_KEYS, (r + 1) * PEER_KEYS)
        thr_rows = [thr_ref[h, pl.ds(r0 + r, 1), :] for h in range(PEER_HEADS)]
        e1_rows = [e1_ref[h, pl.ds(r0 + r, 1), :] for h in range(PEER_HEADS)]
        for lt in range(tb // LANE):
            ls = slice(lt * LANE, (lt + 1) * LANE)
            w = None
            for h in range(PEER_HEADS):
                term = jnp.where(s2_ref[h, :, ls] >= thr_rows[h][:, ls], e1_rows[h][:, ls] * e2_ref[h, :, ls], 0.0)
                w = term if w is None else w + term
            a_ref[rs, ls] = (w * _gelu_exact(ht_ref[rs, ls])).astype(BF16)


def _peer_kernel(x_ref, sh_ref, sc_ref, wqt_ref, k1_ref, k2_ref, u_ref, vt_ref, o_ref,
                 xb_ref, thr_ref, s2_ref, e1_ref, e2_ref, ht_ref, a_ref, acc_ref, *, rows):
    e = pl.program_id(2)
    tabs = (thr_ref, s2_ref, e1_ref, e2_ref)

    @pl.when(e == 0)
    def _first():
        _peer_prepare(x_ref, sh_ref, sc_ref, wqt_ref, k1_ref, k2_ref, xb_ref, *tabs)
        acc_ref[...] = jnp.zeros_like(acc_ref)

    ht_ref[...] = lax.dot_general(u_ref[...], xb_ref[...], NT, preferred_element_type=F32)
    _peer_weights(e, rows, *tabs, ht_ref, a_ref)
    acc_ref[...] += jnp.dot(vt_ref[...], a_ref[...], preferred_element_type=F32)

    @pl.when(e == pl.num_programs(2) - 1)
    def _finish():
        o_ref[...] = acc_ref[...].T


def peer(x, shift, scale, wqt, k1, k2, u, vt, *, ctx_blocks, tb=256, eb=1024):
    bsz, t, d = x.shape
    n = u.shape[0]
    rows = eb // PEER_KEYS
    row = lambda b, j, e: (b, j, 0)
    sel = lambda b, j, e: (b, jnp.minimum(j // ctx_blocks, 1), 0, 0)
    const2 = lambda b, j, e: (0, 0)
    const3 = lambda b, j, e: (0, 0, 0)
    tab = pltpu.VMEM((PEER_HEADS, PEER_KEYS, tb), F32)
    return pl.pallas_call(
        functools.partial(_peer_kernel, rows=rows),
        grid=(bsz, t // tb, n // eb),
        in_specs=[pl.BlockSpec((None, tb, d), row),
                  pl.BlockSpec((None, None, 1, d), sel), pl.BlockSpec((None, None, 1, d), sel),
                  pl.BlockSpec(wqt.shape, const2),
                  pl.BlockSpec(k1.shape, const3), pl.BlockSpec(k2.shape, const3),
                  pl.BlockSpec((eb, d), lambda b, j, e: (e, 0)),
                  pl.BlockSpec((d, eb), lambda b, j, e: (0, e))],
        out_specs=pl.BlockSpec((None, tb, d), row),
        out_shape=jax.ShapeDtypeStruct((bsz, t, d), F32),
        scratch_shapes=[pltpu.VMEM((tb, d), BF16), tab, tab, tab, tab,
                        pltpu.VMEM((eb, tb), F32), pltpu.VMEM((eb, tb), BF16), pltpu.VMEM((d, tb), F32)],
        compiler_params=_cparams(("parallel", "parallel", "arbitrary")),
        name="peer",
    )(x, shift, scale, wqt, k1, k2, u, vt)


def _bwd_seg(s, cs, nseg):
    return jnp.where(s < cs, cs - 1 - s, nseg - 1 - (s - cs))


def _ret_kernel(qf_ref, kf_ref, vf_ref, qb_ref, kb_ref, vb_ref, d_ref, qdec_ref, kdec_ref, cdec_ref,
                of_ref, ob_ref, s_ref, *, chunk, nchunk):
    @pl.when(pl.program_id(2) == 0)
    def _init():
        s_ref[...] = jnp.zeros_like(s_ref)

    cdec = cdec_ref[...]
    ins = ((qf_ref, kf_ref, vf_ref, of_ref), (qb_ref, kb_ref, vb_ref, ob_ref))

    def step(i, carry):
        new = []
        for d in range(2):
            q_ref, k_ref, v_ref, o_ref = ins[d]
            c = i if d == 0 else nchunk - 1 - i
            s = carry[d]
            sl = pl.ds(pl.multiple_of(c * chunk, chunk), chunk)
            q = q_ref[sl, :]
            k = k_ref[sl, :]
            v = v_ref[sl, :]
            scores = lax.dot_general(q, k, NT, preferred_element_type=F32) * d_ref[d]
            intra = jnp.dot(scores.astype(BF16), v, preferred_element_type=F32)
            qd = (q.astype(F32) * qdec_ref[d]).astype(BF16)
            inter = jnp.dot(qd, s.astype(BF16), preferred_element_type=F32)
            o_ref[sl, :] = intra + inter
            kd = (k.astype(F32) * kdec_ref[d]).astype(BF16)
            new.append(s * cdec + lax.dot_general(kd, v, TN, preferred_element_type=F32))
        return tuple(new)

    s0, s1 = lax.fori_loop(0, nchunk, step, (s_ref[0], s_ref[1]))
    s_ref[0] = s0
    s_ref[1] = s1


def retention_scan(q, k, v, dmat, qdec, kdec, cdec, *, chunk, seg, ctx_segs):
    bsz, h, t, dk = q.shape
    dv = v.shape[-1]
    assert t % seg == 0 and seg % chunk == 0
    nseg = t // seg
    fwd = lambda d_: pl.BlockSpec((None, None, seg, d_), lambda b, hh, s: (b, hh, s, 0))
    bwd = lambda d_: pl.BlockSpec((None, None, seg, d_), lambda b, hh, s: (b, hh, _bwd_seg(s, ctx_segs, nseg), 0))
    per_head = lambda *shape: pl.BlockSpec((2, None) + shape, lambda b, hh, s: (0, hh, 0, 0))
    return pl.pallas_call(
        functools.partial(_ret_kernel, chunk=chunk, nchunk=seg // chunk),
        grid=(bsz, h, nseg),
        in_specs=[fwd(dk), fwd(dk), fwd(dv), bwd(dk), bwd(dk), bwd(dv),
                  per_head(chunk, chunk), per_head(chunk, dk), per_head(chunk, dk),
                  pl.BlockSpec((None, dk, dv), lambda b, hh, s: (hh, 0, 0))],
        out_specs=[fwd(dv), bwd(dv)],
        out_shape=[jax.ShapeDtypeStruct((bsz, h, t, dv), F32)] * 2,
        scratch_shapes=[pltpu.VMEM((2, dk, dv), F32)],
        compiler_params=_cparams(("parallel", "parallel", "arbitrary")),
        name="retention",
    )(q, k, v, q, k, v, dmat, qdec, kdec, cdec)


def _dn_kernel(qf_ref, kf_ref, vf_ref, qb_ref, kb_ref, vb_ref, aux_ref, rows_ref, of_ref, ob_ref,
               s_ref, m_s, b_s, qp_s, cd_s, sall_s, *, chunk, nchunk, eps):
    C = chunk

    @pl.when(pl.program_id(2) == 0)
    def _init():
        s_ref[...] = jnp.zeros_like(s_ref)

    ri = lax.broadcasted_iota(jnp.int32, (C, C), 0)
    ci = lax.broadcasted_iota(jnp.int32, (C, C), 1)
    eye = (ri == ci).astype(F32)
    bmm = lambda a, b: jnp.einsum('nij,njk->nik', a.astype(BF16), b.astype(BF16), preferred_element_type=F32)
    bmm_nt = lambda a, b: jnp.einsum('nid,njd->nij', a.astype(BF16), b.astype(BF16),
                                     preferred_element_type=F32)
    bmm_tn = lambda a, b: jnp.einsum('nci,ncj->nij', a.astype(BF16), b.astype(BF16),
                                     preferred_element_type=F32)
    ins = ((qf_ref, kf_ref, vf_ref, of_ref), (qb_ref, kb_ref, vb_ref, ob_ref))

    for d in range(2):
        q_ref, k_ref, v_ref, o_ref = ins[d]
        incl = (ri >= ci) if d == 0 else (ri <= ci)
        strict = (ri > ci) if d == 0 else (ri < ci)
        dk = q_ref.shape[-1]
        q = q_ref[...]
        k = k_ref[...]
        q = (q * (lax.rsqrt(jnp.sum(q * q, axis=-1, keepdims=True) + eps) * dk ** -0.5)).reshape(nchunk, C, dk)
        k = (k * lax.rsqrt(jnp.sum(k * k, axis=-1, keepdims=True) + eps)).reshape(nchunk, C, dk)
        v = v_ref[...].reshape(nchunk, C, v_ref.shape[-1])
        aux = aux_ref[d].reshape(nchunk, C, aux_ref.shape[-1])
        gcol = aux[:, :, 0:1]
        bcol = aux[:, :, 1:2]
        grow = rows_ref[d, :, 0:1, :]
        brow = rows_ref[d, :, 1:2, :]
        kk = bmm_nt(k, k)
        qk = bmm_nt(q, k)
        decay = jnp.where(incl, jnp.exp(jnp.where(incl, gcol - grow, 0.0)), 0.0)
        x = jnp.where(strict, -(kk * decay * bcol), 0.0)
        x2 = bmm(x, x)
        x4 = bmm(x2, x2)
        x8 = bmm(x4, x4)
        x16 = bmm(x8, x8)
        x32 = bmm(x16, x16)
        p1 = eye + x + x2 + bmm(x, x2)
        p2 = eye + x4 + x8 + bmm(x4, x8)
        p3 = eye + x16 + x32 + bmm(x16, x32)
        tinv = bmm(bmm(p1, p2), p3)
        egrow = jnp.exp(grow)
        u = bmm(tinv * brow, v)
        w = bmm(tinv * (brow * egrow), k)
        attn = qk * decay
        gtot = grow[:, :, C - 1:C] if d == 0 else grow[:, :, 0:1]
        qd = q * jnp.exp(gcol)
        kd = k * jnp.exp(gtot - gcol)
        m_s[d] = bmm_tn(kd, w).astype(BF16)
        b_s[d] = bmm_tn(kd, u)
        qp_s[d] = (qd - bmm(attn, w)).astype(BF16)
        o_ref[...] = bmm(attn, u).reshape(o_ref.shape)
        cd_s[d] = jnp.broadcast_to(jnp.exp(gtot), cd_s.shape[1:])

    def step(i, carry):
        new = []
        for d in range(2):
            c = i if d == 0 else nchunk - 1 - i
            s = carry[d]
            sb = s.astype(BF16)
            sall_s[d, c] = sb
            new.append(s * cd_s[d, c] - jnp.dot(m_s[d, c], sb, preferred_element_type=F32) + b_s[d, c])
        return tuple(new)

    s0, s1 = lax.fori_loop(0, nchunk, step, (s_ref[0], s_ref[1]))
    s_ref[0] = s0
    s_ref[1] = s1

    for d in range(2):
        o_ref = ins[d][3]
        o_ref[...] += bmm(qp_s[d], sall_s[d]).reshape(o_ref.shape)


def deltanet_scan(qkv, aux, rows, *, heads, dk, dv, chunk, seg, ctx_segs, eps=1e-6):
    bsz, t, _ = qkv.shape
    assert dk == dv == LANE and t % seg == 0 and seg % chunk == 0
    nseg = t // seg
    nchunk = seg // chunk
    fwd = lambda off: pl.BlockSpec((None, seg, dk), lambda b, hh, s: (b, s, off + hh))
    bwd = lambda off: pl.BlockSpec((None, seg, dk), lambda b, hh, s: (b, _bwd_seg(s, ctx_segs, nseg), off + hh))
    return pl.pallas_call(
        functools.partial(_dn_kernel, chunk=chunk, nchunk=nchunk, eps=eps),
        grid=(bsz, heads, nseg),
        in_specs=[fwd(0), fwd(heads), fwd(2 * heads), bwd(0), bwd(heads), bwd(2 * heads),
                  pl.BlockSpec((2, None, None, seg, aux.shape[-1]), lambda b, hh, s: (0, b, hh, s, 0)),
                  pl.BlockSpec((2, None, None, nchunk, 8, chunk), lambda b, hh, s: (0, b, hh, s, 0, 0))],
        out_specs=[fwd(0), bwd(0)],
        out_shape=[jax.ShapeDtypeStruct((bsz, t, heads * dv), F32)] * 2,
        scratch_shapes=[pltpu.VMEM((2, dk, dv), F32),
                        pltpu.VMEM((2, nchunk, dk, dv), BF16),
                        pltpu.VMEM((2, nchunk, dk, dv), F32),
                        pltpu.VMEM((2, nchunk, chunk, dk), BF16),
                        pltpu.VMEM((2, nchunk, 1, dv), F32),
                        pltpu.VMEM((2, nchunk, dk, dv), BF16)],
        compiler_params=_cparams(("parallel", "parallel", "arbitrary")),
        name="deltanet",
    )(qkv, qkv, qkv, qkv, qkv, qkv, aux, rows)


def _rope_tables(length, dim, ctx_len):
    rows = length // GRID_W
    n_freq = dim // 4
    inv = ROPE_BASE ** (-jnp.arange(n_freq, dtype=F32) / n_freq)
    r = jnp.repeat(jnp.arange(rows, dtype=F32), GRID_W)
    c = jnp.tile(jnp.arange(GRID_W, dtype=F32), rows)
    ang = jnp.concatenate([r[:, None] * inv, c[:, None] * inv], axis=-1)
    cos = jnp.concatenate([jnp.ones((ctx_len, dim // 2), F32), jnp.cos(ang)], axis=0)
    sin = jnp.concatenate([jnp.zeros((ctx_len, dim // 2), F32), jnp.sin(ang)], axis=0)
    return cos, sin


def _apply_rope(x, cos, sin):
    x1, x2 = jnp.split(x, 2, axis=-1)
    return jnp.concatenate([x1 * cos - x2 * sin, x2 * cos + x1 * sin], axis=-1)


def _retention(rq, rk, rv, cos, sin, lc, seg):
    b, t, _ = rq.shape
    q = _apply_rope(rq.reshape(b, t, RET_HEADS, RET_DK), cos[:, None], sin[:, None])
    k = _apply_rope(rk.reshape(b, t, RET_HEADS, RET_DK), cos[:, None], sin[:, None]) * RET_DK ** -0.5
    v = rv.reshape(b, t, RET_HEADS, RET_DV)
    tr = lambda a: jnp.transpose(a, (0, 2, 1, 3)).astype(BF16)
    lg = jnp.log1p(-jnp.exp2(-5.0 - jnp.arange(RET_HEADS, dtype=F32)))[:, None, None]
    C = RET_CHUNK
    pos = jnp.arange(C, dtype=F32)
    dist = pos[:, None] - pos[None, :]
    dm = lambda dd, mask: jnp.where(mask, jnp.exp(jnp.where(mask, dd, 0.0) * lg), 0.0)
    dmat = jnp.stack([dm(dist, dist >= 0), dm(-dist, dist < 0)])
    col = lambda e: jnp.broadcast_to(jnp.exp(e * lg[:, :, 0])[..., None], (RET_HEADS, C, RET_DK))
    qdec = jnp.stack([col(pos + 1.0), col(C - pos)])
    kdec = jnp.stack([col(C - 1.0 - pos), col(pos)])
    cdec = jnp.broadcast_to(jnp.exp(C * lg), (RET_HEADS, RET_DK, RET_DV))
    o_f, o_b = retention_scan(tr(q), tr(k), tr(v), dmat, qdec, kdec, cdec, chunk=C, seg=seg, ctx_segs=lc // seg)
    return o_f + o_b


def _group_norm_heads(o, g, eps=1e-5):
    oc = o - jnp.mean(o, -1, keepdims=True)
    y = oc * lax.rsqrt(jnp.mean(oc * oc, -1, keepdims=True) + eps)
    b, h, L, dv = o.shape
    return jnp.transpose(y, (0, 2, 1, 3)).reshape(b, L, h * dv) * g


def _conv_silu(x, w, lc):
    b, t, ch = x.shape
    K = w.shape[0]
    half = (K - 1) // 2
    xp = jnp.pad(x, ((0, 0), (half, K // 2), (0, 0)))
    pos = jnp.arange(t)
    acc = 0.0
    for j in range(K):
        src = pos + (j - half)
        ok = (src >= 0) & (src < t) & ((pos < lc) == (src < lc))
        acc = acc + jnp.where(ok[None, :, None], xp[:, j:j + t], 0.0) * w[j]
    return acc * jax.nn.sigmoid(acc)


def _deltanet(qkv, ab, conv_w, a_log, dt_bias, lc, seg):
    b, t, _ = qkv.shape
    H, C = DN_HEADS, DN_CHUNK
    act = _conv_silu(qkv, conv_w, lc)
    a = ab[..., :2 * H].reshape(b, t, 2, H)
    bt = ab[..., 2 * H:4 * H].reshape(b, t, 2, H)
    g = jnp.transpose(-jnp.exp(a_log) * jax.nn.softplus(a + dt_bias), (2, 0, 3, 1))
    beta = jnp.transpose(jax.nn.sigmoid(bt), (2, 0, 3, 1))
    gch = g.reshape(2, b, H, t // C, C)
    pre = jnp.cumsum(gch, axis=-1)
    gc = jnp.stack([pre[0], jnp.sum(gch[1], -1, keepdims=True) - pre[1] + gch[1]])
    bc = beta.reshape(2, b, H, t // C, C)
    rows = jnp.pad(jnp.stack([gc, bc], axis=-2), ((0, 0),) * 4 + ((0, 6), (0, 0)))
    aux = jnp.pad(jnp.stack([gc.reshape(2, b, H, t), beta], axis=-1), ((0, 0),) * 4 + ((0, 6),))
    nseg, cs = t // seg, lc // seg

    def visit_order(a_, per_seg):
        shp = a_.shape
        a_ = a_.reshape(shp[:2] + (nseg, per_seg) + shp[3:])
        a_ = jnp.concatenate([jnp.flip(a_[:, :, :cs], 2), jnp.flip(a_[:, :, cs:], 2)], axis=2)
        return a_.reshape(shp)

    rows = jnp.stack([rows[0], visit_order(rows[1], seg // C)])
    aux = jnp.stack([aux[0], visit_order(aux[1], seg)])
    return deltanet_scan(act, aux, rows, heads=H, dk=DN_DK, dv=DN_DV, chunk=C, seg=seg, ctx_segs=cs)


def _pad_cols(w, width):
    return jnp.pad(w, ((0, 0), (0, width - w.shape[1])))


def kernel(x, c, ctx, c_ctx, ada_w, ada_b, ln1_g, ln1_b, ln2_g, ln2_b, ar_w_in, mla_q_norm, mla_w_uq,
           mla_kv_norm, mla_w_ukv, ret_gn_g, ar_w_out, dn_w_in, dn_conv, dn_a_log, dn_dt_bias, dn_norm_g,
           dn_w_out, peer_w_q, peer_k1, peer_k2, peer_u, peer_v):
    bsz, L, D = x.shape
    lc = ctx.shape[1]
    T = lc + L
    tm = 256
    assert lc % tm == 0 and L % tm == 0
    cb = lc // tm
    seg = tm

    X = jnp.concatenate([ctx, x], axis=1)
    cc = jnp.zeros((8, D), F32).at[:bsz].set(c).at[bsz].set(c_ctx)
    mod_all = ada_all(cc, ada_w, ada_b)

    cos_m, sin_m = _rope_tables(L, MLA_ROPE, lc)
    cos_r, sin_r = _rope_tables(L, RET_DK, lc)

    for l in range(DEPTH):
        j = l // 2
        mod = mod_all[l]
        ml = mod[:bsz].reshape(bsz, N_MOD, D)
        mc = jnp.broadcast_to(mod[bsz].reshape(1, N_MOD, D), (bsz, N_MOD, D))
        msel = jnp.stack([mc, ml], axis=1)
        mvec = [msel[:, :, i][:, :, None, :] for i in range(N_MOD)]
        sh1, sc1, g1, sh2, sc2, g2 = mvec

        if l % 2 == 0:
            w_in = ar_w_in[j]
            w_pad = jnp.concatenate(
                [w_in[:, :416], jnp.zeros((D, 96), F32), w_in[:, 416:]], axis=1).astype(BF16)
            splits = [(0, 256), (256, 384), (384, 512), (512, 1024), (1024, 1536), (1536, 2048), (2048, 2560)]
            cq, ckv, krp, rq, rk, rv, rg = proj(X, w_pad, splits, mode="mod", shift=sh1, scale=sc1,
                                                ctx_blocks=cb, tm=tm, name="ar_in")
            (qf,) = proj(cq, mla_w_uq[j].astype(BF16), [(0, MLA_HEADS * (MLA_NOPE + MLA_ROPE))], mode="rms",
                         gain=mla_q_norm[j], tm=tm, name="mla_uq")
            (kvf,) = proj(ckv, mla_w_ukv[j].astype(BF16), [(0, MLA_HEADS * (MLA_NOPE + MLA_V))], mode="rms",
                          gain=mla_kv_norm[j], tm=tm, name="mla_ukv")
            qf = qf.reshape(bsz, T, MLA_HEADS, MLA_NOPE + MLA_ROPE)
            kvf = kvf.reshape(bsz, T, MLA_HEADS, MLA_NOPE + MLA_V)
            qn, qr = qf[..., :MLA_NOPE], qf[..., MLA_NOPE:]
            kn, vv = kvf[..., :MLA_NOPE], kvf[..., MLA_NOPE:]
            qr = _apply_rope(qr, cos_m[:, None], sin_m[:, None])
            kr = _apply_rope(krp[..., :MLA_ROPE], cos_m, sin_m)
            qh = jnp.transpose(jnp.concatenate([qn, qr], -1) * MLA_SCALE, (0, 2, 1, 3)).astype(BF16)
            kh = jnp.transpose(jnp.concatenate(
                [kn, jnp.broadcast_to(kr[:, :, None, :], (bsz, T, MLA_HEADS, MLA_ROPE))], -1), (0, 2, 1, 3)).astype(BF16)
            vh = jnp.transpose(vv, (0, 2, 1, 3)).astype(BF16)
            o_l = attention(qh[:, :, lc:], kh, vh, tq=512, tk=768)
            o_c = attention(qh[:, :, :lc], kh[:, :, :lc], vh[:, :, :lc], tq=lc, tk=lc)
            mla = jnp.transpose(jnp.concatenate([o_c, o_l], axis=2), (0, 2, 1, 3)).reshape(bsz, T, MLA_HEADS * MLA_V)
            ro = _retention(rq, rk, rv, cos_r, sin_r, lc, seg)
            ret = _group_norm_heads(ro, ret_gn_g[j]) * jax.nn.silu(rg)
            mix = jnp.concatenate([mla, ret], axis=-1)
            (y,) = proj(mix, ar_w_out[j].astype(BF16), [(0, D)], tm=tm, name="ar_out")
        else:
            w_pad = _pad_cols(dn_w_in[j], 4224).astype(BF16)
            splits = [(0, 3072), (3072, 4096), (4096, 4224)]
            qkv, gate, ab = proj(X, w_pad, splits, mode="mod", shift=sh1, scale=sc1, ctx_blocks=cb, tm=tm,
                                 name="dn_in")
            o_f, o_b = _deltanet(qkv, ab, dn_conv[j], dn_a_log[j], dn_dt_bias[j], lc, seg)
            (y,) = proj(o_f, dn_w_out[j].astype(BF16), [(0, D)], mode="gated_rms", other=o_b, gate=gate,
                        gain=dn_norm_g[j], group=DN_DV, tm=tm, name="dn_out")

        X = resid_ln(X, y, g1, ln1_g[l], ln1_b[l], ctx_blocks=cb, tm=tm)
        f = peer(X, sh2, sc2, peer_w_q[l].T.astype(BF16), peer_k1[l].astype(BF16), peer_k2[l].astype(BF16),
                 peer_u[l].astype(BF16), peer_v[l].T.astype(BF16), ctx_blocks=cb)
        X = resid_ln(X, f, g2, ln2_g[l], ln2_b[l], ctx_blocks=cb, tm=tm)

    return X[:, lc:]
```

```python
import functools
import math

import numpy as np
import jax
import jax.numpy as jnp
from jax import lax
from jax.experimental import pallas as pl
from jax.experimental.pallas import tpu as pltpu

F32 = jnp.float32
BF16 = jnp.bfloat16

DEPTH = 4
GRID_W = 64
ROPE_BASE = 10000.0
N_MOD = 6

MLA_HEADS = 8
MLA_Q_RANK = 256
MLA_KV_RANK = 128
MLA_NOPE = 64
MLA_ROPE = 32
MLA_V = 64
MLA_SCALE = (MLA_NOPE + MLA_ROPE) ** -0.5

RET_HEADS = 8
RET_DK = 64
RET_DV = 64
RET_CHUNK = 128

DN_HEADS = 8
DN_DK = 128
DN_DV = 128
DN_CONV = 5
DN_CHUNK = 64

PEER_HEADS = 8
PEER_KEYS = 128
PEER_QDIM = 256
PEER_TOPK = 16
PEER_SUB = 4

DEEPNORM_ALPHA = (2 * DEPTH) ** 0.25

LANE = 128
VMEM_LIMIT = 56 * 1024 * 1024

NT = (((1,), (1,)), ((), ()))
TN = (((0,), (0,)), ((), ()))


def _cparams(sem):
    return pltpu.CompilerParams(dimension_semantics=sem, vmem_limit_bytes=VMEM_LIMIT)


def _ada_kernel(c_ref, w_ref, b_ref, o_ref):
    c = c_ref[...]
    a = (c * jax.nn.sigmoid(c)).astype(BF16)
    o_ref[...] = jnp.dot(a, w_ref[...].astype(BF16), preferred_element_type=F32) + b_ref[...]


def ada_all(cc, ada_w, ada_b, tn=1024):
    depth, d, n = ada_w.shape
    m = cc.shape[0]
    return pl.pallas_call(
        _ada_kernel,
        grid=(depth, n // tn),
        in_specs=[
            pl.BlockSpec((m, d), lambda l, j: (0, 0)),
            pl.BlockSpec((None, d, tn), lambda l, j: (l, 0, j)),
            pl.BlockSpec((None, 1, tn), lambda l, j: (l, 0, j)),
        ],
        out_specs=pl.BlockSpec((None, m, tn), lambda l, j: (l, 0, j)),
        out_shape=jax.ShapeDtypeStruct((depth, m, n), F32),
        compiler_params=_cparams(("arbitrary", "arbitrary")),
        name="ada",
    )(cc, ada_w, ada_b.reshape(depth, 1, n))


def _proj_kernel(*refs, mode, splits, eps, group):
    if mode == "mod":
        x_ref, sh_ref, sc_ref, w_ref = refs[:4]
        outs = refs[4:]
        x = x_ref[...] * (1.0 + sc_ref[...]) + sh_ref[...]
    elif mode == "rms":
        x_ref, g_ref, w_ref = refs[:3]
        outs = refs[3:]
        x = x_ref[...]
        x = x * lax.rsqrt(jnp.mean(x * x, axis=-1, keepdims=True) + eps) * g_ref[...]
    elif mode == "gated_rms":
        a_ref, b_ref, gate_ref, g_ref, w_ref = refs[:5]
        outs = refs[5:]
        o = a_ref[...] + b_ref[...]
        gate = gate_ref[...]
        parts = []
        for h in range(o.shape[1] // group):
            oh = o[:, h * group:(h + 1) * group]
            parts.append(oh * lax.rsqrt(jnp.mean(oh * oh, axis=-1, keepdims=True) + eps))
        x = jnp.concatenate(parts, axis=1) * g_ref[...] * (gate * jax.nn.sigmoid(gate))
    else:
        x_ref, w_ref = refs[:2]
        outs = refs[2:]
        x = x_ref[...]
    z = jnp.dot(x.astype(BF16), w_ref[...], preferred_element_type=F32)
    for o_ref, (a, b) in zip(outs, splits):
        o_ref[...] = z[:, a:b].astype(o_ref.dtype)


def proj(x, w, splits, *, mode="none", shift=None, scale=None, gain=None, other=None, gate=None, group=LANE,
         ctx_blocks=1, tm=256, eps=1e-6, name="proj"):
    bsz, t, k = x.shape
    n = w.shape[1]
    row = lambda b, j: (b, j, 0)
    sel = lambda b, j: (b, jnp.minimum(j // ctx_blocks, 1), 0, 0)
    in_specs = [pl.BlockSpec((None, tm, k), row)]
    args = [x]
    if mode == "mod":
        in_specs += [pl.BlockSpec((None, None, 1, k), sel)] * 2
        args += [shift, scale]
    elif mode == "rms":
        in_specs += [pl.BlockSpec((1, k), lambda b, j: (0, 0))]
        args += [gain.reshape(1, k)]
    elif mode == "gated_rms":
        in_specs += [pl.BlockSpec((None, tm, k), row)] * 2 + [pl.BlockSpec((1, k), lambda b, j: (0, 0))]
        args += [other, gate, jnp.tile(gain, k // group).reshape(1, k)]
    in_specs += [pl.BlockSpec((k, n), lambda b, j: (0, 0))]
    args += [w]
    return pl.pallas_call(
        functools.partial(_proj_kernel, mode=mode, splits=tuple(splits), eps=eps, group=group),
        grid=(bsz, t // tm),
        in_specs=in_specs,
        out_specs=[pl.BlockSpec((None, tm, b - a), row) for a, b in splits],
        out_shape=[jax.ShapeDtypeStruct((bsz, t, b - a), F32) for a, b in splits],
        compiler_params=_cparams(("parallel", "parallel")),
        name=name,
    )(*args)


def _resid_ln_kernel(x_ref, y_ref, gate_ref, g_ref, b_ref, o_ref, *, eps):
    v = DEEPNORM_ALPHA * x_ref[...] + gate_ref[...] * y_ref[...]
    vc = v - jnp.mean(v, axis=-1, keepdims=True)
    var = jnp.mean(vc * vc, axis=-1, keepdims=True)
    o_ref[...] = vc * lax.rsqrt(var + eps) * g_ref[...] + b_ref[...]


def resid_ln(x, y, gate, g, b, *, ctx_blocks=1, tm=256, eps=1e-5):
    bsz, t, d = x.shape
    row = lambda b_, j: (b_, j, 0)
    sel = lambda b_, j: (b_, jnp.minimum(j // ctx_blocks, 1), 0, 0)
    vec = lambda b_, j: (0, 0)
    return pl.pallas_call(
        functools.partial(_resid_ln_kernel, eps=eps),
        grid=(bsz, t // tm),
        in_specs=[pl.BlockSpec((None, tm, d), row), pl.BlockSpec((None, tm, d), row),
                  pl.BlockSpec((None, None, 1, d), sel),
                  pl.BlockSpec((1, d), vec), pl.BlockSpec((1, d), vec)],
        out_specs=pl.BlockSpec((None, tm, d), row),
        out_shape=jax.ShapeDtypeStruct((bsz, t, d), F32),
        compiler_params=_cparams(("parallel", "parallel")),
        name="resid_ln",
    )(x, y, gate, g.reshape(1, d), b.reshape(1, d))


def _attn_kernel(q_ref, k_ref, v_ref, o_ref, *, tk, nk, hb):
    tq = q_ref.shape[1]
    dv = v_ref.shape[-1]
    qs = [q_ref[h] for h in range(hb)]

    def body(i, carry):
        start = pl.multiple_of(i * tk, tk)
        ss = [lax.dot_general(qs[h], k_ref[h, pl.ds(start, tk), :], NT, preferred_element_type=F32)
              for h in range(hb)]
        ps, new = [], []
        for h in range(hb):
            m, l, acc = carry[h]
            m_new = jnp.maximum(m, jnp.max(ss[h], axis=1, keepdims=True))
            p = jnp.exp(ss[h] - m_new)
            alpha = jnp.exp(m - m_new)
            ps.append(p.astype(BF16))
            new.append((m_new, alpha * l + jnp.sum(p, axis=1, keepdims=True), alpha * acc))
        out = []
        for h in range(hb):
            m_new, l, acc = new[h]
            out.append((m_new, l, acc + jnp.dot(ps[h], v_ref[h, pl.ds(start, tk), :], preferred_element_type=F32)))
        return tuple(out)

    init = tuple((jnp.full((tq, 1), -jnp.inf, F32), jnp.zeros((tq, 1), F32), jnp.zeros((tq, dv), F32))
                 for _ in range(hb))
    fin = lax.fori_loop(0, nk, body, init)
    for h in range(hb):
        _, l, acc = fin[h]
        o_ref[h] = acc / l


def attention(q, k, v, *, tq, tk, hb=2):
    bsz, h, lq, dq = q.shape
    lk, dv = v.shape[2], v.shape[3]
    assert lq % tq == 0 and lk % tk == 0 and h % hb == 0
    return pl.pallas_call(
        functools.partial(_attn_kernel, tk=tk, nk=lk // tk, hb=hb),
        grid=(bsz, h // hb, lq // tq),
        in_specs=[pl.BlockSpec((None, hb, tq, dq), lambda b, hh, i: (b, hh, i, 0)),
                  pl.BlockSpec((None, hb, lk, dq), lambda b, hh, i: (b, hh, 0, 0)),
                  pl.BlockSpec((None, hb, lk, dv), lambda b, hh, i: (b, hh, 0, 0))],
        out_specs=pl.BlockSpec((None, hb, tq, dv), lambda b, hh, i: (b, hh, i, 0)),
        out_shape=jax.ShapeDtypeStruct((bsz, h, lq, dv), F32),
        compiler_params=_cparams(("parallel", "parallel", "parallel")),
        name="mla_attn",
    )(q, k, v)


def _top_values(s, k):
    vals = []
    cur = s
    for i in range(k):
        m = jnp.max(cur, axis=0, keepdims=True)
        vals.append(m)
        if i + 1 < k:
            cur = jnp.where(cur >= m, -jnp.inf, cur)
    return vals


def _gelu_exact(x):
    return 0.5 * x * (1.0 + lax.erf(x * (2.0 ** -0.5)))


def _peer_prepare(x_ref, sh_ref, sc_ref, wqt_ref, k1_ref, k2_ref, xb_ref, thr_ref, s2_ref, e1_ref, e2_ref):
    half = PEER_QDIM // 2
    K = PEER_TOPK
    H = PEER_HEADS
    x = x_ref[...] * (1.0 + sc_ref[...]) + sh_ref[...]
    xb = x.astype(BF16)
    xb_ref[...] = xb
    qt = lax.dot_general(wqt_ref[...], xb, NT, preferred_element_type=F32)
    v1s, v2s = [], []
    for h in range(H):
        q1 = qt[h * PEER_QDIM: h * PEER_QDIM + half].astype(BF16)
        q2 = qt[h * PEER_QDIM + half: (h + 1) * PEER_QDIM].astype(BF16)
        s1 = jnp.dot(k1_ref[h], q1, preferred_element_type=F32)
        s2 = jnp.dot(k2_ref[h], q2, preferred_element_type=F32)
        thr_ref[h] = s1
        s2_ref[h] = s2
        v1s.append(_top_values(s1, K + 1))
        v2s.append(_top_values(s2, K + 1))
    v1 = [jnp.concatenate([v1s[h][a] for h in range(H)], axis=0) for a in range(K + 1)]
    v2 = [jnp.concatenate([v2s[h][b] for h in range(H)], axis=0) for b in range(K + 1)]
    cands = [v1[a] + v2[b] for a in range(K + 1) for b in range(K + 1) if (a + 1) * (b + 1) <= K + 1]
    cur = list(cands)
    for i in range(K + 1):
        m = functools.reduce(jnp.maximum, cur)
        if i == K - 1:
            kth = m
        if i < K:
            cur = [jnp.where(c >= m, -jnp.inf, c) for c in cur]
    tau = 0.5 * (kth + m)
    top = cands[0]
    z = functools.reduce(jnp.add, [jnp.where(c >= tau, jnp.exp(c - top), 0.0) for c in cands])
    for h in range(H):
        s1 = thr_ref[h]
        thr_ref[h] = tau[h:h + 1] - s1
        e1_ref[h] = jnp.exp(s1 - v1s[h][0]) / z[h:h + 1]
        e2_ref[h] = jnp.exp(s2_ref[h] - v2s[h][0])


def _peer_weights(blk, rows, row_lo, row_hi, thr_ref, s2_ref, e1_ref, e2_ref, ht_ref, a_ref):
    tb = ht_ref.shape[1]
    r0 = blk * rows
    for r in range(row_lo, row_hi):
        rs = slice(r * PEER_KEYS, (r + 1) * PEER_KEYS)
        thr_rows = [thr_ref[h, pl.ds(r0 + r, 1), :] for h in range(PEER_HEADS)]
        e1_rows = [e1_ref[h, pl.ds(r0 + r, 1), :] for h in range(PEER_HEADS)]
        for lt in range(tb // LANE):
            ls = slice(lt * LANE, (lt + 1) * LANE)
            w = None
            for h in range(PEER_HEADS):
                term = jnp.where(s2_ref[h, :, ls] >= thr_rows[h][:, ls], e1_rows[h][:, ls] * e2_ref[h, :, ls], 0.0)
                w = term if w is None else w + term
            a_ref[rs, ls] = (w * _gelu_exact(ht_ref[rs, ls])).astype(BF16)


def _peer_kernel(x_ref, sh_ref, sc_ref, wqt_ref, k1_ref, k2_ref, u_ref, vt_ref, o_ref,
                 xb_ref, thr_ref, s2_ref, e1_ref, e2_ref, ht_ref, a_ref, acc_ref, *, rows):
    e = pl.program_id(2)
    tabs = (thr_ref, s2_ref, e1_ref, e2_ref)

    @pl.when(e == 0)
    def _first():
        _peer_prepare(x_ref, sh_ref, sc_ref, wqt_ref, k1_ref, k2_ref, xb_ref, *tabs)
        acc_ref[...] = jnp.zeros_like(acc_ref)

    per = rows // PEER_SUB
    sub = lambda i: slice(i * per * PEER_KEYS, (i + 1) * per * PEER_KEYS)
    h_sub = lambda i: lax.dot_general(u_ref[sub(i), :], xb_ref[...], NT, preferred_element_type=F32)
    ht_ref[sub(0), :] = h_sub(0)
    for sb in range(PEER_SUB):
        if sb + 1 < PEER_SUB:
            ht_ref[sub(sb + 1), :] = h_sub(sb + 1)
        _peer_weights(e, rows, sb * per, (sb + 1) * per, *tabs, ht_ref, a_ref)
        acc_ref[...] += jnp.dot(vt_ref[:, sub(sb)], a_ref[sub(sb), :], preferred_element_type=F32)

    @pl.when(e == pl.num_programs(2) - 1)
    def _finish():
        o_ref[...] = acc_ref[...].T


def peer(x, shift, scale, wqt, k1, k2, u, vt, *, ctx_blocks, tb=256, eb=1024):
    bsz, t, d = x.shape
    n = u.shape[0]
    rows = eb // PEER_KEYS
    row = lambda b, j, e: (b, j, 0)
    sel = lambda b, j, e: (b, jnp.minimum(j // ctx_blocks, 1), 0, 0)
    const2 = lambda b, j, e: (0, 0)
    const3 = lambda b, j, e: (0, 0, 0)
    tab = pltpu.VMEM((PEER_HEADS, PEER_KEYS, tb), F32)
    return pl.pallas_call(
        functools.partial(_peer_kernel, rows=rows),
        grid=(bsz, t // tb, n // eb),
        in_specs=[pl.BlockSpec((None, tb, d), row),
                  pl.BlockSpec((None, None, 1, d), sel), pl.BlockSpec((None, None, 1, d), sel),
                  pl.BlockSpec(wqt.shape, const2),
                  pl.BlockSpec(k1.shape, const3), pl.BlockSpec(k2.shape, const3),
                  pl.BlockSpec((eb, d), lambda b, j, e: (e, 0)),
                  pl.BlockSpec((d, eb), lambda b, j, e: (0, e))],
        out_specs=pl.BlockSpec((None, tb, d), row),
        out_shape=jax.ShapeDtypeStruct((bsz, t, d), F32),
        scratch_shapes=[pltpu.VMEM((tb, d), BF16), tab, tab, tab, tab,
                        pltpu.VMEM((eb, tb), F32), pltpu.VMEM((eb, tb), BF16), pltpu.VMEM((d, tb), F32)],
        compiler_params=_cparams(("parallel", "parallel", "arbitrary")),
        name="peer",
    )(x, shift, scale, wqt, k1, k2, u, vt)


def _bwd_seg(s, cs, nseg):
    return jnp.where(s < cs, cs - 1 - s, nseg - 1 - (s - cs))


def _ret_kernel(qf_ref, kf_ref, vf_ref, qb_ref, kb_ref, vb_ref, d_ref, qdec_ref, kdec_ref, cdec_ref,
                of_ref, ob_ref, s_ref, *, chunk, nchunk):
    @pl.when(pl.program_id(2) == 0)
    def _init():
        s_ref[...] = jnp.zeros_like(s_ref)

    cdec = cdec_ref[...]
    ins = ((qf_ref, kf_ref, vf_ref, of_ref), (qb_ref, kb_ref, vb_ref, ob_ref))

    def step(i, carry):
        new = []
        for d in range(2):
            q_ref, k_ref, v_ref, o_ref = ins[d]
            c = i if d == 0 else nchunk - 1 - i
            s = carry[d]
            sl = pl.ds(pl.multiple_of(c * chunk, chunk), chunk)
            q = q_ref[sl, :]
            k = k_ref[sl, :]
            v = v_ref[sl, :]
            scores = lax.dot_general(q, k, NT, preferred_element_type=F32) * d_ref[d]
            intra = jnp.dot(scores.astype(BF16), v, preferred_element_type=F32)
            qd = (q.astype(F32) * qdec_ref[d]).astype(BF16)
            inter = jnp.dot(qd, s.astype(BF16), preferred_element_type=F32)
            o_ref[sl, :] = intra + inter
            kd = (k.astype(F32) * kdec_ref[d]).astype(BF16)
            new.append(s * cdec + lax.dot_general(kd, v, TN, preferred_element_type=F32))
        return tuple(new)

    s0, s1 = lax.fori_loop(0, nchunk, step, (s_ref[0], s_ref[1]))
    s_ref[0] = s0
    s_ref[1] = s1


def retention_scan(q, k, v, dmat, qdec, kdec, cdec, *, chunk, seg, ctx_segs):
    bsz, h, t, dk = q.shape
    dv = v.shape[-1]
    assert t % seg == 0 and seg % chunk == 0
    nseg = t // seg
    fwd = lambda d_: pl.BlockSpec((None, None, seg, d_), lambda b, hh, s: (b, hh, s, 0))
    bwd = lambda d_: pl.BlockSpec((None, None, seg, d_), lambda b, hh, s: (b, hh, _bwd_seg(s, ctx_segs, nseg), 0))
    per_head = lambda *shape: pl.BlockSpec((2, None) + shape, lambda b, hh, s: (0, hh, 0, 0))
    return pl.pallas_call(
        functools.partial(_ret_kernel, chunk=chunk, nchunk=seg // chunk),
        grid=(bsz, h, nseg),
        in_specs=[fwd(dk), fwd(dk), fwd(dv), bwd(dk), bwd(dk), bwd(dv),
                  per_head(chunk, chunk), per_head(chunk, dk), per_head(chunk, dk),
                  pl.BlockSpec((None, dk, dv), lambda b, hh, s: (hh, 0, 0))],
        out_specs=[fwd(dv), bwd(dv)],
        out_shape=[jax.ShapeDtypeStruct((bsz, h, t, dv), F32)] * 2,
        scratch_shapes=[pltpu.VMEM((2, dk, dv), F32)],
        compiler_params=_cparams(("parallel", "parallel", "arbitrary")),
        name="retention",
    )(q, k, v, q, k, v, dmat, qdec, kdec, cdec)


def _dn_kernel(qf_ref, kf_ref, vf_ref, qb_ref, kb_ref, vb_ref, aux_ref, rows_ref, of_ref, ob_ref,
               s_ref, m_s, b_s, qp_s, cd_s, sall_s, *, chunk, nchunk, hb, eps):
    C = chunk

    @pl.when(pl.program_id(2) == 0)
    def _init():
        s_ref[...] = jnp.zeros_like(s_ref)

    ri = lax.broadcasted_iota(jnp.int32, (C, C), 0)
    ci = lax.broadcasted_iota(jnp.int32, (C, C), 1)
    eye = (ri == ci).astype(F32)
    bmm = lambda a, b: jnp.einsum('nij,njk->nik', a.astype(BF16), b.astype(BF16), preferred_element_type=F32)
    bmm_nt = lambda a, b: jnp.einsum('nid,njd->nij', a.astype(BF16), b.astype(BF16),
                                     preferred_element_type=F32)
    bmm_tn = lambda a, b: jnp.einsum('nci,ncj->nij', a.astype(BF16), b.astype(BF16),
                                     preferred_element_type=F32)
    ins = ((qf_ref, kf_ref, vf_ref, of_ref), (qb_ref, kb_ref, vb_ref, ob_ref))
    dk = qf_ref.shape[-1] // hb
    dv = vf_ref.shape[-1] // hb
    chains = [(d, j) for d in range(2) for j in range(hb)]
    nb = len(chains) * nchunk

    def gather(fn):
        return jnp.concatenate([fn(d, j) for d, j in chains], axis=0)

    def l2n(ref, j, scale):
        t = ref[:, j * dk:(j + 1) * dk]
        return (t * (lax.rsqrt(jnp.sum(t * t, axis=-1, keepdims=True) + eps) * scale)).reshape(nchunk, C, dk)

    q = gather(lambda d, j: l2n(ins[d][0], j, dk ** -0.5))
    k = gather(lambda d, j: l2n(ins[d][1], j, 1.0))
    v = gather(lambda d, j: ins[d][2][:, j * dv:(j + 1) * dv].reshape(nchunk, C, dv))
    aux = gather(lambda d, j: aux_ref[d, j].reshape(nchunk, C, aux_ref.shape[-1]))
    rws = gather(lambda d, j: rows_ref[d, j])
    gcol = aux[:, :, 0:1]
    bcol = aux[:, :, 1:2]
    grow = rws[:, 0:1, :]
    brow = rws[:, 1:2, :]
    is_fwd = lax.broadcasted_iota(jnp.int32, (nb, 1, 1), 0) < hb * nchunk
    signed = (ri - ci) * jnp.where(is_fwd, 1, -1)
    incl = signed >= 0
    strict = signed > 0
    kk = bmm_nt(k, k)
    qk = bmm_nt(q, k)
    decay = jnp.where(incl, jnp.exp(jnp.where(incl, gcol - grow, 0.0)), 0.0)
    x = jnp.where(strict, -(kk * decay * bcol), 0.0)
    x2 = bmm(x, x)
    x4 = bmm(x2, x2)
    x8 = bmm(x4, x4)
    x16 = bmm(x8, x8)
    x32 = bmm(x16, x16)
    p1 = eye + x + x2 + bmm(x, x2)
    p2 = eye + x4 + x8 + bmm(x4, x8)
    p3 = eye + x16 + x32 + bmm(x16, x32)
    tinv = bmm(bmm(p1, p2), p3)
    egrow = jnp.exp(grow)
    u = bmm(tinv * brow, v)
    w = bmm(tinv * (brow * egrow), k)
    attn = qk * decay
    gtot = jnp.where(is_fwd, grow[:, :, C - 1:C], grow[:, :, 0:1])
    qd = q * jnp.exp(gcol)
    kd = k * jnp.exp(gtot - gcol)
    per_chain = lambda t: t.reshape((2, hb, nchunk) + t.shape[1:])
    m_s[...] = per_chain(bmm_tn(kd, w).astype(BF16))
    b_s[...] = per_chain(bmm_tn(kd, u))
    qp_s[...] = per_chain((qd - bmm(attn, w)).astype(BF16))
    cd_s[...] = per_chain(jnp.broadcast_to(jnp.exp(gtot), (nb, 1, dv)))
    o_local = bmm(attn, u)

    def step(i, carry):
        new = []
        for (d, j), s in zip(chains, carry):
            c = i if d == 0 else nchunk - 1 - i
            sb = s.astype(BF16)
            sall_s[d, j, c] = sb
            new.append(s * cd_s[d, j, c] - jnp.dot(m_s[d, j, c], sb, preferred_element_type=F32) + b_s[d, j, c])
        return tuple(new)

    final = lax.fori_loop(0, nchunk, step, tuple(s_ref[d, j] for d, j in chains))
    for (d, j), s in zip(chains, final):
        s_ref[d, j] = s

    o_all = o_local + bmm(qp_s[...].reshape(nb, C, dk), sall_s[...].reshape(nb, dk, dv))
    for idx, (d, j) in enumerate(chains):
        ins[d][3][:, j * dv:(j + 1) * dv] = o_all[idx * nchunk:(idx + 1) * nchunk].reshape(nchunk * C, dv)


def deltanet_scan(qkv, aux, rows, *, heads, dk, dv, chunk, seg, ctx_segs, hb=4, eps=1e-6):
    bsz, t, _ = qkv.shape
    assert dk == dv == LANE and t % seg == 0 and seg % chunk == 0 and heads % hb == 0
    nseg = t // seg
    nchunk = seg // chunk
    hg = heads // hb
    fwd = lambda off: pl.BlockSpec((None, seg, hb * dk), lambda b, hh, s: (b, s, off + hh))
    bwd = lambda off: pl.BlockSpec((None, seg, hb * dk), lambda b, hh, s: (b, _bwd_seg(s, ctx_segs, nseg), off + hh))
    return pl.pallas_call(
        functools.partial(_dn_kernel, chunk=chunk, nchunk=nchunk, hb=hb, eps=eps),
        grid=(bsz, hg, nseg),
        in_specs=[fwd(0), fwd(hg), fwd(2 * hg), bwd(0), bwd(hg), bwd(2 * hg),
                  pl.BlockSpec((2, None, hb, seg, aux.shape[-1]), lambda b, hh, s: (0, b, hh, s, 0)),
                  pl.BlockSpec((2, None, hb, nchunk, 8, chunk), lambda b, hh, s: (0, b, hh, s, 0, 0))],
        out_specs=[fwd(0), bwd(0)],
        out_shape=[jax.ShapeDtypeStruct((bsz, t, heads * dv), F32)] * 2,
        scratch_shapes=[pltpu.VMEM((2, hb, dk, dv), F32),
                        pltpu.VMEM((2, hb, nchunk, dk, dv), BF16),
                        pltpu.VMEM((2, hb, nchunk, dk, dv), F32),
                        pltpu.VMEM((2, hb, nchunk, chunk, dk), BF16),
                        pltpu.VMEM((2, hb, nchunk, 1, dv), F32),
                        pltpu.VMEM((2, hb, nchunk, dk, dv), BF16)],
        compiler_params=_cparams(("parallel", "parallel", "arbitrary")),
        name="deltanet",
    )(qkv, qkv, qkv, qkv, qkv, qkv, aux, rows)


def _rope_tables(length, dim, ctx_len):
    rows = length // GRID_W
    n_freq = dim // 4
    inv = ROPE_BASE ** (-jnp.arange(n_freq, dtype=F32) / n_freq)
    r = jnp.repeat(jnp.arange(rows, dtype=F32), GRID_W)
    c = jnp.tile(jnp.arange(GRID_W, dtype=F32), rows)
    ang = jnp.concatenate([r[:, None] * inv, c[:, None] * inv], axis=-1)
    cos = jnp.concatenate([jnp.ones((ctx_len, dim // 2), F32), jnp.cos(ang)], axis=0)
    sin = jnp.concatenate([jnp.zeros((ctx_len, dim // 2), F32), jnp.sin(ang)], axis=0)
    return cos, sin


def _apply_rope(x, cos, sin):
    x1, x2 = jnp.split(x, 2, axis=-1)
    return jnp.concatenate([x1 * cos - x2 * sin, x2 * cos + x1 * sin], axis=-1)


def _retention(rq, rk, rv, cos, sin, lc, seg):
    b, t, _ = rq.shape
    q = _apply_rope(rq.reshape(b, t, RET_HEADS, RET_DK), cos[:, None], sin[:, None])
    k = _apply_rope(rk.reshape(b, t, RET_HEADS, RET_DK), cos[:, None], sin[:, None]) * RET_DK ** -0.5
    v = rv.reshape(b, t, RET_HEADS, RET_DV)
    tr = lambda a: jnp.transpose(a, (0, 2, 1, 3)).astype(BF16)
    lg = jnp.log1p(-jnp.exp2(-5.0 - jnp.arange(RET_HEADS, dtype=F32)))[:, None, None]
    C = RET_CHUNK
    pos = jnp.arange(C, dtype=F32)
    dist = pos[:, None] - pos[None, :]
    dm = lambda dd, mask: jnp.where(mask, jnp.exp(jnp.where(mask, dd, 0.0) * lg), 0.0)
    dmat = jnp.stack([dm(dist, dist >= 0), dm(-dist, dist < 0)])
    col = lambda e: jnp.broadcast_to(jnp.exp(e * lg[:, :, 0])[..., None], (RET_HEADS, C, RET_DK))
    qdec = jnp.stack([col(pos + 1.0), col(C - pos)])
    kdec = jnp.stack([col(C - 1.0 - pos), col(pos)])
    cdec = jnp.broadcast_to(jnp.exp(C * lg), (RET_HEADS, RET_DK, RET_DV))
    o_f, o_b = retention_scan(tr(q), tr(k), tr(v), dmat, qdec, kdec, cdec, chunk=C, seg=seg, ctx_segs=lc // seg)
    return o_f + o_b


def _group_norm_heads(o, g, eps=1e-5):
    oc = o - jnp.mean(o, -1, keepdims=True)
    y = oc * lax.rsqrt(jnp.mean(oc * oc, -1, keepdims=True) + eps)
    b, h, L, dv = o.shape
    return jnp.transpose(y, (0, 2, 1, 3)).reshape(b, L, h * dv) * g


def _conv_silu(x, w, lc):
    b, t, ch = x.shape
    K = w.shape[0]
    half = (K - 1) // 2
    xp = jnp.pad(x, ((0, 0), (half, K // 2), (0, 0)))
    pos = jnp.arange(t)
    acc = 0.0
    for j in range(K):
        src = pos + (j - half)
        ok = (src >= 0) & (src < t) & ((pos < lc) == (src < lc))
        acc = acc + jnp.where(ok[None, :, None], xp[:, j:j + t], 0.0) * w[j]
    return acc * jax.nn.sigmoid(acc)


def _deltanet(qkv, ab, conv_w, a_log, dt_bias, lc, seg):
    b, t, _ = qkv.shape
    H, C = DN_HEADS, DN_CHUNK
    act = _conv_silu(qkv, conv_w, lc)
    a = ab[..., :2 * H].reshape(b, t, 2, H)
    bt = ab[..., 2 * H:4 * H].reshape(b, t, 2, H)
    g = jnp.transpose(-jnp.exp(a_log) * jax.nn.softplus(a + dt_bias), (2, 0, 3, 1))
    beta = jnp.transpose(jax.nn.sigmoid(bt), (2, 0, 3, 1))
    gch = g.reshape(2, b, H, t // C, C)
    pre = jnp.cumsum(gch, axis=-1)
    gc = jnp.stack([pre[0], jnp.sum(gch[1], -1, keepdims=True) - pre[1] + gch[1]])
    bc = beta.reshape(2, b, H, t // C, C)
    rows = jnp.pad(jnp.stack([gc, bc], axis=-2), ((0, 0),) * 4 + ((0, 6), (0, 0)))
    aux = jnp.pad(jnp.stack([gc.reshape(2, b, H, t), beta], axis=-1), ((0, 0),) * 4 + ((0, 6),))
    nseg, cs = t // seg, lc // seg

    def visit_order(a_, per_seg):
        shp = a_.shape
        a_ = a_.reshape(shp[:2] + (nseg, per_seg) + shp[3:])
        a_ = jnp.concatenate([jnp.flip(a_[:, :, :cs], 2), jnp.flip(a_[:, :, cs:], 2)], axis=2)
        return a_.reshape(shp)

    rows = jnp.stack([rows[0], visit_order(rows[1], seg // C)])
    aux = jnp.stack([aux[0], visit_order(aux[1], seg)])
    return deltanet_scan(act, aux, rows, heads=H, dk=DN_DK, dv=DN_DV, chunk=C, seg=seg, ctx_segs=cs)


def _pad_cols(w, width):
    return jnp.pad(w, ((0, 0), (0, width - w.shape[1])))


def kernel(x, c, ctx, c_ctx, ada_w, ada_b, ln1_g, ln1_b, ln2_g, ln2_b, ar_w_in, mla_q_norm, mla_w_uq,
           mla_kv_norm, mla_w_ukv, ret_gn_g, ar_w_out, dn_w_in, dn_conv, dn_a_log, dn_dt_bias, dn_norm_g,
           dn_w_out, peer_w_q, peer_k1, peer_k2, peer_u, peer_v):
    bsz, L, D = x.shape
    lc = ctx.shape[1]
    T = lc + L
    tm = 256
    assert lc % tm == 0 and L % tm == 0
    cb = lc // tm
    seg = tm

    X = jnp.concatenate([ctx, x], axis=1)
    cc = jnp.zeros((8, D), F32).at[:bsz].set(c).at[bsz].set(c_ctx)
    mod_all = ada_all(cc, ada_w, ada_b)

    cos_m, sin_m = _rope_tables(L, MLA_ROPE, lc)
    cos_r, sin_r = _rope_tables(L, RET_DK, lc)

    for l in range(DEPTH):
        j = l // 2
        mod = mod_all[l]
        ml = mod[:bsz].reshape(bsz, N_MOD, D)
        mc = jnp.broadcast_to(mod[bsz].reshape(1, N_MOD, D), (bsz, N_MOD, D))
        msel = jnp.stack([mc, ml], axis=1)
        mvec = [msel[:, :, i][:, :, None, :] for i in range(N_MOD)]
        sh1, sc1, g1, sh2, sc2, g2 = mvec

        if l % 2 == 0:
            w_in = ar_w_in[j]
            w_pad = jnp.concatenate(
                [w_in[:, :416], jnp.zeros((D, 96), F32), w_in[:, 416:]], axis=1).astype(BF16)
            splits = [(0, 256), (256, 384), (384, 512), (512, 1024), (1024, 1536), (1536, 2048), (2048, 2560)]
            cq, ckv, krp, rq, rk, rv, rg = proj(X, w_pad, splits, mode="mod", shift=sh1, scale=sc1,
                                                ctx_blocks=cb, tm=tm, name="ar_in")
            (qf,) = proj(cq, mla_w_uq[j].astype(BF16), [(0, MLA_HEADS * (MLA_NOPE + MLA_ROPE))], mode="rms",
                         gain=mla_q_norm[j], tm=tm, name="mla_uq")
            (kvf,) = proj(ckv, mla_w_ukv[j].astype(BF16), [(0, MLA_HEADS * (MLA_NOPE + MLA_V))], mode="rms",
                          gain=mla_kv_norm[j], tm=tm, name="mla_ukv")
            qf = qf.reshape(bsz, T, MLA_HEADS, MLA_NOPE + MLA_ROPE)
            kvf = kvf.reshape(bsz, T, MLA_HEADS, MLA_NOPE + MLA_V)
            qn, qr = qf[..., :MLA_NOPE], qf[..., MLA_NOPE:]
            kn, vv = kvf[..., :MLA_NOPE], kvf[..., MLA_NOPE:]
            qr = _apply_rope(qr, cos_m[:, None], sin_m[:, None])
            kr = _apply_rope(krp[..., :MLA_ROPE], cos_m, sin_m)
            qh = jnp.transpose(jnp.concatenate([qn, qr], -1) * MLA_SCALE, (0, 2, 1, 3)).astype(BF16)
            kh = jnp.transpose(jnp.concatenate(
                [kn, jnp.broadcast_to(kr[:, :, None, :], (bsz, T, MLA_HEADS, MLA_ROPE))], -1), (0, 2, 1, 3)).astype(BF16)
            vh = jnp.transpose(vv, (0, 2, 1, 3)).astype(BF16)
            tk = next(c for c in (1408, 768, lc) if T % c == 0)
            o_l = attention(qh[:, :, lc:], kh, vh, tq=512, tk=tk)
            o_c = attention(qh[:, :, :lc], kh[:, :, :lc], vh[:, :, :lc], tq=lc, tk=lc)
            mla = jnp.transpose(jnp.concatenate([o_c, o_l], axis=2), (0, 2, 1, 3)).reshape(bsz, T, MLA_HEADS * MLA_V)
            ro = _retention(rq, rk, rv, cos_r, sin_r, lc, seg)
            ret = _group_norm_heads(ro, ret_gn_g[j]) * jax.nn.silu(rg)
            mix = jnp.concatenate([mla, ret], axis=-1)
            (y,) = proj(mix, ar_w_out[j].astype(BF16), [(0, D)], tm=tm, name="ar_out")
        else:
            w_pad = _pad_cols(dn_w_in[j], 4224).astype(BF16)
            splits = [(0, 3072), (3072, 4096), (4096, 4224)]
            qkv, gate, ab = proj(X, w_pad, splits, mode="mod", shift=sh1, scale=sc1, ctx_blocks=cb, tm=tm,
                                 name="dn_in")
            o_f, o_b = _deltanet(qkv, ab, dn_conv[j], dn_a_log[j], dn_dt_bias[j], lc, seg)
            (y,) = proj(o_f, dn_w_out[j].astype(BF16), [(0, D)], mode="gated_rms", other=o_b, gate=gate,
                        gain=dn_norm_g[j], group=DN_DV, tm=tm, name="dn_out")

        X = resid_ln(X, y, g1, ln1_g[l], ln1_b[l], ctx_blocks=cb, tm=tm)
        f = peer(X, sh2, sc2, peer_w_q[l].T.astype(BF16), peer_k1[l].astype(BF16), peer_k2[l].astype(BF16),
                 peer_u[l].astype(BF16), peer_v[l].T.astype(BF16), ctx_blocks=cb)
        X = resid_ln(X, f, g2, ln2_g[l], ln2_b[l], ctx_blocks=cb, tm=tm)

    return X[:, lc:]
```

```python
import functools
import math

import numpy as np
import jax
import jax.numpy as jnp
from jax import lax
from jax.experimental import pallas as pl
from jax.experimental.pallas import tpu as pltpu

F32 = jnp.float32
BF16 = jnp.bfloat16

DEPTH = 4
GRID_W = 64
ROPE_BASE = 10000.0
N_MOD = 6

MLA_HEADS = 8
MLA_Q_RANK = 256
MLA_KV_RANK = 128
MLA_NOPE = 64
MLA_ROPE = 32
MLA_V = 64
MLA_SCALE = (MLA_NOPE + MLA_ROPE) ** -0.5

RET_HEADS = 8
RET_DK = 64
RET_DV = 64
RET_CHUNK = 128

DN_HEADS = 8
DN_DK = 128
DN_DV = 128
DN_CONV = 5
DN_CHUNK = 64

PEER_HEADS = 8
PEER_KEYS = 128
PEER_QDIM = 256
PEER_TOPK = 16

DEEPNORM_ALPHA = (2 * DEPTH) ** 0.25

LANE = 128
VMEM_LIMIT = 56 * 1024 * 1024

NT = (((1,), (1,)), ((), ()))
TN = (((0,), (0,)), ((), ()))


def _cparams(sem):
    return pltpu.CompilerParams(dimension_semantics=sem, vmem_limit_bytes=VMEM_LIMIT)


def _ada_kernel(c_ref, w_ref, b_ref, o_ref):
    c = c_ref[...]
    a = (c * jax.nn.sigmoid(c)).astype(BF16)
    o_ref[...] = jnp.dot(a, w_ref[...].astype(BF16), preferred_element_type=F32) + b_ref[...]


def ada_all(cc, ada_w, ada_b, tn=1024):
    depth, d, n = ada_w.shape
    m = cc.shape[0]
    return pl.pallas_call(
        _ada_kernel,
        grid=(depth, n // tn),
        in_specs=[
            pl.BlockSpec((m, d), lambda l, j: (0, 0)),
            pl.BlockSpec((None, d, tn), lambda l, j: (l, 0, j)),
            pl.BlockSpec((None, 1, tn), lambda l, j: (l, 0, j)),
        ],
        out_specs=pl.BlockSpec((None, m, tn), lambda l, j: (l, 0, j)),
        out_shape=jax.ShapeDtypeStruct((depth, m, n), F32),
        compiler_params=_cparams(("arbitrary", "arbitrary")),
        name="ada",
    )(cc, ada_w, ada_b.reshape(depth, 1, n))


def _proj_kernel(*refs, mode, splits, eps, group):
    if mode == "mod":
        x_ref, sh_ref, sc_ref, w_ref = refs[:4]
        outs = refs[4:]
        x = x_ref[...] * (1.0 + sc_ref[...]) + sh_ref[...]
    elif mode == "rms":
        x_ref, g_ref, w_ref = refs[:3]
        outs = refs[3:]
        x = x_ref[...]
        x = x * lax.rsqrt(jnp.mean(x * x, axis=-1, keepdims=True) + eps) * g_ref[...]
    elif mode == "gated_rms":
        a_ref, b_ref, gate_ref, g_ref, w_ref = refs[:5]
        outs = refs[5:]
        o = a_ref[...] + b_ref[...]
        gate = gate_ref[...]
        parts = []
        for h in range(o.shape[1] // group):
            oh = o[:, h * group:(h + 1) * group]
            parts.append(oh * lax.rsqrt(jnp.mean(oh * oh, axis=-1, keepdims=True) + eps))
        x = jnp.concatenate(parts, axis=1) * g_ref[...] * (gate * jax.nn.sigmoid(gate))
    else:
        x_ref, w_ref = refs[:2]
        outs = refs[2:]
        x = x_ref[...]
    z = jnp.dot(x.astype(BF16), w_ref[...], preferred_element_type=F32)
    for o_ref, (a, b) in zip(outs, splits):
        o_ref[...] = z[:, a:b].astype(o_ref.dtype)


def proj(x, w, splits, *, mode="none", shift=None, scale=None, gain=None, other=None, gate=None, group=LANE,
         ctx_blocks=1, tm=256, eps=1e-6, name="proj"):
    bsz, t, k = x.shape
    n = w.shape[1]
    row = lambda b, j: (b, j, 0)
    sel = lambda b, j: (b, jnp.minimum(j // ctx_blocks, 1), 0, 0)
    in_specs = [pl.BlockSpec((None, tm, k), row)]
    args = [x]
    if mode == "mod":
        in_specs += [pl.BlockSpec((None, None, 1, k), sel)] * 2
        args += [shift, scale]
    elif mode == "rms":
        in_specs += [pl.BlockSpec((1, k), lambda b, j: (0, 0))]
        args += [gain.reshape(1, k)]
    elif mode == "gated_rms":
        in_specs += [pl.BlockSpec((None, tm, k), row)] * 2 + [pl.BlockSpec((1, k), lambda b, j: (0, 0))]
        args += [other, gate, jnp.tile(gain, k // group).reshape(1, k)]
    in_specs += [pl.BlockSpec((k, n), lambda b, j: (0, 0))]
    args += [w]
    return pl.pallas_call(
        functools.partial(_proj_kernel, mode=mode, splits=tuple(splits), eps=eps, group=group),
        grid=(bsz, t // tm),
        in_specs=in_specs,
        out_specs=[pl.BlockSpec((None, tm, b - a), row) for a, b in splits],
        out_shape=[jax.ShapeDtypeStruct((bsz, t, b - a), F32) for a, b in splits],
        compiler_params=_cparams(("parallel", "parallel")),
        name=name,
    )(*args)


def _resid_ln_kernel(x_ref, y_ref, gate_ref, g_ref, b_ref, o_ref, *, eps):
    v = DEEPNORM_ALPHA * x_ref[...] + gate_ref[...] * y_ref[...]
    vc = v - jnp.mean(v, axis=-1, keepdims=True)
    var = jnp.mean(vc * vc, axis=-1, keepdims=True)
    o_ref[...] = vc * lax.rsqrt(var + eps) * g_ref[...] + b_ref[...]


def resid_ln(x, y, gate, g, b, *, ctx_blocks=1, tm=256, eps=1e-5):
    bsz, t, d = x.shape
    row = lambda b_, j: (b_, j, 0)
    sel = lambda b_, j: (b_, jnp.minimum(j // ctx_blocks, 1), 0, 0)
    vec = lambda b_, j: (0, 0)
    return pl.pallas_call(
        functools.partial(_resid_ln_kernel, eps=eps),
        grid=(bsz, t // tm),
        in_specs=[pl.BlockSpec((None, tm, d), row), pl.BlockSpec((None, tm, d), row),
                  pl.BlockSpec((None, None, 1, d), sel),
                  pl.BlockSpec((1, d), vec), pl.BlockSpec((1, d), vec)],
        out_specs=pl.BlockSpec((None, tm, d), row),
        out_shape=jax.ShapeDtypeStruct((bsz, t, d), F32),
        compiler_params=_cparams(("parallel", "parallel")),
        name="resid_ln",
    )(x, y, gate, g.reshape(1, d), b.reshape(1, d))


def _attn_kernel(q_ref, k_ref, v_ref, o_ref, *, tk, nk, hb):
    tq = q_ref.shape[1]
    dv = v_ref.shape[-1]
    qs = [q_ref[h] for h in range(hb)]

    def body(i, carry):
        start = pl.multiple_of(i * tk, tk)
        ss = [lax.dot_general(qs[h], k_ref[h, pl.ds(start, tk), :], NT, preferred_element_type=F32)
              for h in range(hb)]
        ps, new = [], []
        for h in range(hb):
            m, l, acc = carry[h]
            m_new = jnp.maximum(m, jnp.max(ss[h], axis=1, keepdims=True))
            p = jnp.exp(ss[h] - m_new)
            alpha = jnp.exp(m - m_new)
            ps.append(p.astype(BF16))
            new.append((m_new, alpha * l + jnp.sum(p, axis=1, keepdims=True), alpha * acc))
        out = []
        for h in range(hb):
            m_new, l, acc = new[h]
            out.append((m_new, l, acc + jnp.dot(ps[h], v_ref[h, pl.ds(start, tk), :], preferred_element_type=F32)))
        return tuple(out)

    init = tuple((jnp.full((tq, 1), -jnp.inf, F32), jnp.zeros((tq, 1), F32), jnp.zeros((tq, dv), F32))
                 for _ in range(hb))
    fin = lax.fori_loop(0, nk, body, init)
    for h in range(hb):
        _, l, acc = fin[h]
        o_ref[h] = acc / l


def attention(q, k, v, *, tq, tk, hb=2):
    bsz, h, lq, dq = q.shape
    lk, dv = v.shape[2], v.shape[3]
    assert lq % tq == 0 and lk % tk == 0 and h % hb == 0
    return pl.pallas_call(
        functools.partial(_attn_kernel, tk=tk, nk=lk // tk, hb=hb),
        grid=(bsz, h // hb, lq // tq),
        in_specs=[pl.BlockSpec((None, hb, tq, dq), lambda b, hh, i: (b, hh, i, 0)),
                  pl.BlockSpec((None, hb, lk, dq), lambda b, hh, i: (b, hh, 0, 0)),
                  pl.BlockSpec((None, hb, lk, dv), lambda b, hh, i: (b, hh, 0, 0))],
        out_specs=pl.BlockSpec((None, hb, tq, dv), lambda b, hh, i: (b, hh, i, 0)),
        out_shape=jax.ShapeDtypeStruct((bsz, h, lq, dv), F32),
        compiler_params=_cparams(("parallel", "parallel", "parallel")),
        name="mla_attn",
    )(q, k, v)


def _top_values(s, k):
    vals = []
    cur = s
    for i in range(k):
        m = jnp.max(cur, axis=0, keepdims=True)
        vals.append(m)
        if i + 1 < k:
            cur = jnp.where(cur >= m, -jnp.inf, cur)
    return vals


def _gelu_exact(x):
    return 0.5 * x * (1.0 + lax.erf(x * (2.0 ** -0.5)))


def _peer_prepare(x_ref, sh_ref, sc_ref, wqt_ref, k1_ref, k2_ref, xb_ref, thr_ref, s2_ref, e1_ref, e2_ref):
    half = PEER_QDIM // 2
    K = PEER_TOPK
    H = PEER_HEADS
    x = x_ref[...] * (1.0 + sc_ref[...]) + sh_ref[...]
    xb = x.astype(BF16)
    xb_ref[...] = xb
    qt = lax.dot_general(wqt_ref[...], xb, NT, preferred_element_type=F32)
    v1s, v2s = [], []
    for h in range(H):
        q1 = qt[h * PEER_QDIM: h * PEER_QDIM + half].astype(BF16)
        q2 = qt[h * PEER_QDIM + half: (h + 1) * PEER_QDIM].astype(BF16)
        s1 = jnp.dot(k1_ref[h], q1, preferred_element_type=F32)
        s2 = jnp.dot(k2_ref[h], q2, preferred_element_type=F32)
        thr_ref[h] = s1
        s2_ref[h] = s2
        v1s.append(_top_values(s1, K + 1))
        v2s.append(_top_values(s2, K + 1))
    v1 = [jnp.concatenate([v1s[h][a] for h in range(H)], axis=0) for a in range(K + 1)]
    v2 = [jnp.concatenate([v2s[h][b] for h in range(H)], axis=0) for b in range(K + 1)]
    cands = [v1[a] + v2[b] for a in range(K + 1) for b in range(K + 1) if (a + 1) * (b + 1) <= K + 1]
    cur = list(cands)
    for i in range(K + 1):
        m = functools.reduce(jnp.maximum, cur)
        if i == K - 1:
            kth = m
        if i < K:
            cur = [jnp.where(c >= m, -jnp.inf, c) for c in cur]
    tau = 0.5 * (kth + m)
    top = cands[0]
    z = functools.reduce(jnp.add, [jnp.where(c >= tau, jnp.exp(c - top), 0.0) for c in cands])
    for h in range(H):
        s1 = thr_ref[h]
        thr_ref[h] = tau[h:h + 1] - s1
        e1_ref[h] = jnp.exp(s1 - v1s[h][0]) / z[h:h + 1]
        e2_ref[h] = jnp.exp(s2_ref[h] - v2s[h][0])


def _peer_weights(blk, rows, row_lo, row_hi, thr_ref, s2_ref, e1_ref, e2_ref, ht_ref, a_ref):
    tb = ht_ref.shape[1]
    r0 = blk * rows
    for r in range(row_lo, row_hi):
        rs = slice(r * PEER_KEYS, (r + 1) * PEER_KEYS)
        thr_rows = [thr_ref[h, pl.ds(r0 + r, 1), :] for h in range(PEER_HEADS)]
        e1_rows = [e1_ref[h, pl.ds(r0 + r, 1), :] for h in range(PEER_HEADS)]
        for lt in range(tb // LANE):
            ls = slice(lt * LANE, (lt + 1) * LANE)
            w = None
            for h in range(PEER_HEADS):
                term = jnp.where(s2_ref[h, :, ls] >= thr_rows[h][:, ls], e1_rows[h][:, ls] * e2_ref[h, :, ls], 0.0)
                w = term if w is None else w + term
            a_ref[rs, ls] = (w * _gelu_exact(ht_ref[rs, ls])).astype(BF16)


def _peer_kernel(x_ref, sh_ref, sc_ref, wqt_ref, k1_ref, k2_ref, u0_ref, ua_ref, ub_ref, vta_ref, vtb_ref, o_ref,
                 xb_ref, thr_ref, s2_ref, e1_ref, e2_ref, hte_ref, hto_ref, ae_ref, ao_ref, acc_ref, *, rows):
    g = pl.program_id(2)
    last = pl.num_programs(2) - 1
    p = g % 2
    q = 1 - p
    tabs = (thr_ref, s2_ref, e1_ref, e2_ref)

    @pl.when(g == 0)
    def _first():
        _peer_prepare(x_ref, sh_ref, sc_ref, wqt_ref, k1_ref, k2_ref, xb_ref, *tabs)
        acc_ref[...] = jnp.zeros_like(acc_ref)
        ao_ref[1] = jnp.zeros(ao_ref.shape[1:], ao_ref.dtype)
        hte_ref[0] = lax.dot_general(u0_ref[...], xb_ref[...], NT, preferred_element_type=F32)

    @pl.when(g < last)
    def _even():
        hto_ref[...] = lax.dot_general(ua_ref[...], xb_ref[...], NT, preferred_element_type=F32)
        acc_ref[...] += jnp.dot(vta_ref[...], ao_ref[q], preferred_element_type=F32)
        _peer_weights(2 * g, rows, 0, rows, *tabs, hte_ref.at[p], ae_ref)

    @pl.when(g < last)
    def _odd():
        hte_ref[q] = lax.dot_general(ub_ref[...], xb_ref[...], NT, preferred_element_type=F32)
        acc_ref[...] += jnp.dot(vtb_ref[...], ae_ref[...], preferred_element_type=F32)
        _peer_weights(2 * g + 1, rows, 0, rows, *tabs, hto_ref, ao_ref.at[p])

    @pl.when(g == last)
    def _finish():
        acc = acc_ref[...] + jnp.dot(vta_ref[...], ao_ref[q], preferred_element_type=F32)
        o_ref[...] = acc.T


def peer(x, shift, scale, wqt, k1, k2, u, vt, *, ctx_blocks, tb=256, eb=1024):
    bsz, t, d = x.shape
    n = u.shape[0]
    ne = n // eb
    assert ne % 2 == 0
    rows = eb // PEER_KEYS
    row = lambda b, j, g: (b, j, 0)
    sel = lambda b, j, g: (b, jnp.minimum(j // ctx_blocks, 1), 0, 0)
    const2 = lambda b, j, g: (0, 0)
    const3 = lambda b, j, g: (0, 0, 0)
    tab = pltpu.VMEM((PEER_HEADS, PEER_KEYS, tb), F32)
    ublk = lambda f: pl.BlockSpec((eb, d), lambda b, j, g: (f(g), 0))
    vblk = lambda f: pl.BlockSpec((d, eb), lambda b, j, g: (0, f(g)))
    return pl.pallas_call(
        functools.partial(_peer_kernel, rows=rows),
        grid=(bsz, t // tb, ne // 2 + 1),
        in_specs=[pl.BlockSpec((None, tb, d), row),
                  pl.BlockSpec((None, None, 1, d), sel), pl.BlockSpec((None, None, 1, d), sel),
                  pl.BlockSpec(wqt.shape, const2),
                  pl.BlockSpec(k1.shape, const3), pl.BlockSpec(k2.shape, const3),
                  ublk(lambda g: 0),
                  ublk(lambda g: jnp.minimum(2 * g + 1, ne - 1)),
                  ublk(lambda g: jnp.minimum(2 * g + 2, ne - 1)),
                  vblk(lambda g: jnp.maximum(2 * g - 1, 0)),
                  vblk(lambda g: jnp.minimum(2 * g, ne - 1))],
        out_specs=pl.BlockSpec((None, tb, d), row),
        out_shape=jax.ShapeDtypeStruct((bsz, t, d), F32),
        scratch_shapes=[pltpu.VMEM((tb, d), BF16), tab, tab, tab, tab,
                        pltpu.VMEM((2, eb, tb), F32), pltpu.VMEM((eb, tb), F32),
                        pltpu.VMEM((eb, tb), BF16), pltpu.VMEM((2, eb, tb), BF16),
                        pltpu.VMEM((d, tb), F32)],
        compiler_params=_cparams(("parallel", "parallel", "arbitrary")),
        name="peer",
    )(x, shift, scale, wqt, k1, k2, u, u, u, vt, vt)


def _bwd_seg(s, cs, nseg):
    return jnp.where(s < cs, cs - 1 - s, nseg - 1 - (s - cs))


def _ret_kernel(qf_ref, kf_ref, vf_ref, qb_ref, kb_ref, vb_ref, d_ref, qdec_ref, kdec_ref, cdec_ref,
                of_ref, ob_ref, s_ref, *, chunk, nchunk):
    @pl.when(pl.program_id(2) == 0)
    def _init():
        s_ref[...] = jnp.zeros_like(s_ref)

    cdec = cdec_ref[...]
    ins = ((qf_ref, kf_ref, vf_ref, of_ref), (qb_ref, kb_ref, vb_ref, ob_ref))

    def step(i, carry):
        new = []
        for d in range(2):
            q_ref, k_ref, v_ref, o_ref = ins[d]
            c = i if d == 0 else nchunk - 1 - i
            s = carry[d]
            sl = pl.ds(pl.multiple_of(c * chunk, chunk), chunk)
            q = q_ref[sl, :]
            k = k_ref[sl, :]
            v = v_ref[sl, :]
            scores = lax.dot_general(q, k, NT, preferred_element_type=F32) * d_ref[d]
            intra = jnp.dot(scores.astype(BF16), v, preferred_element_type=F32)
            qd = (q.astype(F32) * qdec_ref[d]).astype(BF16)
            inter = jnp.dot(qd, s.astype(BF16), preferred_element_type=F32)
            o_ref[sl, :] = intra + inter
            kd = (k.astype(F32) * kdec_ref[d]).astype(BF16)
            new.append(s * cdec + lax.dot_general(kd, v, TN, preferred_element_type=F32))
        return tuple(new)

    s0, s1 = lax.fori_loop(0, nchunk, step, (s_ref[0], s_ref[1]))
    s_ref[0] = s0
    s_ref[1] = s1


def retention_scan(q, k, v, dmat, qdec, kdec, cdec, *, chunk, seg, ctx_segs):
    bsz, h, t, dk = q.shape
    dv = v.shape[-1]
    assert t % seg == 0 and seg % chunk == 0
    nseg = t // seg
    fwd = lambda d_: pl.BlockSpec((None, None, seg, d_), lambda b, hh, s: (b, hh, s, 0))
    bwd = lambda d_: pl.BlockSpec((None, None, seg, d_), lambda b, hh, s: (b, hh, _bwd_seg(s, ctx_segs, nseg), 0))
    per_head = lambda *shape: pl.BlockSpec((2, None) + shape, lambda b, hh, s: (0, hh, 0, 0))
    return pl.pallas_call(
        functools.partial(_ret_kernel, chunk=chunk, nchunk=seg // chunk),
        grid=(bsz, h, nseg),
        in_specs=[fwd(dk), fwd(dk), fwd(dv), bwd(dk), bwd(dk), bwd(dv),
                  per_head(chunk, chunk), per_head(chunk, dk), per_head(chunk, dk),
                  pl.BlockSpec((None, dk, dv), lambda b, hh, s: (hh, 0, 0))],
        out_specs=[fwd(dv), bwd(dv)],
        out_shape=[jax.ShapeDtypeStruct((bsz, h, t, dv), F32)] * 2,
        scratch_shapes=[pltpu.VMEM((2, dk, dv), F32)],
        compiler_params=_cparams(("parallel", "parallel", "arbitrary")),
        name="retention",
    )(q, k, v, q, k, v, dmat, qdec, kdec, cdec)


def _dn_kernel(qf_ref, kf_ref, vf_ref, qb_ref, kb_ref, vb_ref, aux_ref, rows_ref, of_ref, ob_ref,
               s_ref, m_s, b_s, qp_s, cd_s, sall_s, *, chunk, nchunk, hb, eps):
    C = chunk

    @pl.when(pl.program_id(2) == 0)
    def _init():
        s_ref[...] = jnp.zeros_like(s_ref)

    ri = lax.broadcasted_iota(jnp.int32, (C, C), 0)
    ci = lax.broadcasted_iota(jnp.int32, (C, C), 1)
    eye = (ri == ci).astype(F32)
    bmm = lambda a, b: jnp.einsum('nij,njk->nik', a.astype(BF16), b.astype(BF16), preferred_element_type=F32)
    bmm_nt = lambda a, b: jnp.einsum('nid,njd->nij', a.astype(BF16), b.astype(BF16),
                                     preferred_element_type=F32)
    bmm_tn = lambda a, b: jnp.einsum('nci,ncj->nij', a.astype(BF16), b.astype(BF16),
                                     preferred_element_type=F32)
    ins = ((qf_ref, kf_ref, vf_ref, of_ref), (qb_ref, kb_ref, vb_ref, ob_ref))
    dk = qf_ref.shape[-1] // hb
    dv = vf_ref.shape[-1] // hb
    chains = [(d, j) for d in range(2) for j in range(hb)]
    nb = len(chains) * nchunk

    def gather(fn):
        return jnp.concatenate([fn(d, j) for d, j in chains], axis=0)

    def l2n(ref, j, scale):
        t = ref[:, j * dk:(j + 1) * dk]
        return (t * (lax.rsqrt(jnp.sum(t * t, axis=-1, keepdims=True) + eps) * scale)).reshape(nchunk, C, dk)

    q = gather(lambda d, j: l2n(ins[d][0], j, dk ** -0.5))
    k = gather(lambda d, j: l2n(ins[d][1], j, 1.0))
    v = gather(lambda d, j: ins[d][2][:, j * dv:(j + 1) * dv].reshape(nchunk, C, dv))
    aux = gather(lambda d, j: aux_ref[d, j].reshape(nchunk, C, aux_ref.shape[-1]))
    rws = gather(lambda d, j: rows_ref[d, j])
    gcol = aux[:, :, 0:1]
    bcol = aux[:, :, 1:2]
    grow = rws[:, 0:1, :]
    brow = rws[:, 1:2, :]
    is_fwd = lax.broadcasted_iota(jnp.int32, (nb, 1, 1), 0) < hb * nchunk
    signed = (ri - ci) * jnp.where(is_fwd, 1, -1)
    incl = signed >= 0
    strict = signed > 0
    kk = bmm_nt(k, k)
    qk = bmm_nt(q, k)
    decay = jnp.where(incl, jnp.exp(jnp.where(incl, gcol - grow, 0.0)), 0.0)
    x = jnp.where(strict, -(kk * decay * bcol), 0.0)
    x2 = bmm(x, x)
    x4 = bmm(x2, x2)
    x8 = bmm(x4, x4)
    x16 = bmm(x8, x8)
    x32 = bmm(x16, x16)
    p1 = eye + x + x2 + bmm(x, x2)
    p2 = eye + x4 + x8 + bmm(x4, x8)
    p3 = eye + x16 + x32 + bmm(x16, x32)
    tinv = bmm(bmm(p1, p2), p3)
    egrow = jnp.exp(grow)
    u = bmm(tinv * brow, v)
    w = bmm(tinv * (brow * egrow), k)
    attn = qk * decay
    gtot = jnp.where(is_fwd, grow[:, :, C - 1:C], grow[:, :, 0:1])
    qd = q * jnp.exp(gcol)
    kd = k * jnp.exp(gtot - gcol)
    per_chain = lambda t: t.reshape((2, hb, nchunk) + t.shape[1:])
    m_s[...] = per_chain(bmm_tn(kd, w).astype(BF16))
    b_s[...] = per_chain(bmm_tn(kd, u))
    qp_s[...] = per_chain((qd - bmm(attn, w)).astype(BF16))
    cd_s[...] = per_chain(jnp.broadcast_to(jnp.exp(gtot), (nb, 1, dv)))
    o_local = bmm(attn, u)

    def step(i, carry):
        new = []
        for (d, j), s in zip(chains, carry):
            c = i if d == 0 else nchunk - 1 - i
            sb = s.astype(BF16)
            sall_s[d, j, c] = sb
            new.append(s * cd_s[d, j, c] - jnp.dot(m_s[d, j, c], sb, preferred_element_type=F32) + b_s[d, j, c])
        return tuple(new)

    final = lax.fori_loop(0, nchunk, step, tuple(s_ref[d, j] for d, j in chains))
    for (d, j), s in zip(chains, final):
        s_ref[d, j] = s

    o_all = o_local + bmm(qp_s[...].reshape(nb, C, dk), sall_s[...].reshape(nb, dk, dv))
    for idx, (d, j) in enumerate(chains):
        ins[d][3][:, j * dv:(j + 1) * dv] = o_all[idx * nchunk:(idx + 1) * nchunk].reshape(nchunk * C, dv)


def deltanet_scan(qkv, aux, rows, *, heads, dk, dv, chunk, seg, ctx_segs, hb=4, eps=1e-6):
    bsz, t, _ = qkv.shape
    assert dk == dv == LANE and t % seg == 0 and seg % chunk == 0 and heads % hb == 0
    nseg = t // seg
    nchunk = seg // chunk
    hg = heads // hb
    fwd = lambda off: pl.BlockSpec((None, seg, hb * dk), lambda b, hh, s: (b, s, off + hh))
    bwd = lambda off: pl.BlockSpec((None, seg, hb * dk), lambda b, hh, s: (b, _bwd_seg(s, ctx_segs, nseg), off + hh))
    return pl.pallas_call(
        functools.partial(_dn_kernel, chunk=chunk, nchunk=nchunk, hb=hb, eps=eps),
        grid=(bsz, hg, nseg),
        in_specs=[fwd(0), fwd(hg), fwd(2 * hg), bwd(0), bwd(hg), bwd(2 * hg),
                  pl.BlockSpec((2, None, hb, seg, aux.shape[-1]), lambda b, hh, s: (0, b, hh, s, 0)),
                  pl.BlockSpec((2, None, hb, nchunk, 8, chunk), lambda b, hh, s: (0, b, hh, s, 0, 0))],
        out_specs=[fwd(0), bwd(0)],
        out_shape=[jax.ShapeDtypeStruct((bsz, t, heads * dv), F32)] * 2,
        scratch_shapes=[pltpu.VMEM((2, hb, dk, dv), F32),
                        pltpu.VMEM((2, hb, nchunk, dk, dv), BF16),
                        pltpu.VMEM((2, hb, nchunk, dk, dv), F32),
                        pltpu.VMEM((2, hb, nchunk, chunk, dk), BF16),
                        pltpu.VMEM((2, hb, nchunk, 1, dv), F32),
                        pltpu.VMEM((2, hb, nchunk, dk, dv), BF16)],
        compiler_params=_cparams(("parallel", "parallel", "arbitrary")),
        name="deltanet",
    )(qkv, qkv, qkv, qkv, qkv, qkv, aux, rows)


def _rope_tables(length, dim, ctx_len):
    rows = length // GRID_W
    n_freq = dim // 4
    inv = ROPE_BASE ** (-jnp.arange(n_freq, dtype=F32) / n_freq)
    r = jnp.repeat(jnp.arange(rows, dtype=F32), GRID_W)
    c = jnp.tile(jnp.arange(GRID_W, dtype=F32), rows)
    ang = jnp.concatenate([r[:, None] * inv, c[:, None] * inv], axis=-1)
    cos = jnp.concatenate([jnp.ones((ctx_len, dim // 2), F32), jnp.cos(ang)], axis=0)
    sin = jnp.concatenate([jnp.zeros((ctx_len, dim // 2), F32), jnp.sin(ang)], axis=0)
    return cos, sin


def _apply_rope(x, cos, sin):
    x1, x2 = jnp.split(x, 2, axis=-1)
    return jnp.concatenate([x1 * cos - x2 * sin, x2 * cos + x1 * sin], axis=-1)


def _retention(rq, rk, rv, cos, sin, lc, seg):
    b, t, _ = rq.shape
    q = _apply_rope(rq.reshape(b, t, RET_HEADS, RET_DK), cos[:, None], sin[:, None])
    k = _apply_rope(rk.reshape(b, t, RET_HEADS, RET_DK), cos[:, None], sin[:, None]) * RET_DK ** -0.5
    v = rv.reshape(b, t, RET_HEADS, RET_DV)
    tr = lambda a: jnp.transpose(a, (0, 2, 1, 3)).astype(BF16)
    lg = jnp.log1p(-jnp.exp2(-5.0 - jnp.arange(RET_HEADS, dtype=F32)))[:, None, None]
    C = RET_CHUNK
    pos = jnp.arange(C, dtype=F32)
    dist = pos[:, None] - pos[None, :]
    dm = lambda dd, mask: jnp.where(mask, jnp.exp(jnp.where(mask, dd, 0.0) * lg), 0.0)
    dmat = jnp.stack([dm(dist, dist >= 0), dm(-dist, dist < 0)])
    col = lambda e: jnp.broadcast_to(jnp.exp(e * lg[:, :, 0])[..., None], (RET_HEADS, C, RET_DK))
    qdec = jnp.stack([col(pos + 1.0), col(C - pos)])
    kdec = jnp.stack([col(C - 1.0 - pos), col(pos)])
    cdec = jnp.broadcast_to(jnp.exp(C * lg), (RET_HEADS, RET_DK, RET_DV))
    o_f, o_b = retention_scan(tr(q), tr(k), tr(v), dmat, qdec, kdec, cdec, chunk=C, seg=seg, ctx_segs=lc // seg)
    return o_f + o_b


def _group_norm_heads(o, g, eps=1e-5):
    oc = o - jnp.mean(o, -1, keepdims=True)
    y = oc * lax.rsqrt(jnp.mean(oc * oc, -1, keepdims=True) + eps)
    b, h, L, dv = o.shape
    return jnp.transpose(y, (0, 2, 1, 3)).reshape(b, L, h * dv) * g


def _conv_silu(x, w, lc):
    b, t, ch = x.shape
    K = w.shape[0]
    half = (K - 1) // 2
    xp = jnp.pad(x, ((0, 0), (half, K // 2), (0, 0)))
    pos = jnp.arange(t)
    acc = 0.0
    for j in range(K):
        src = pos + (j - half)
        ok = (src >= 0) & (src < t) & ((pos < lc) == (src < lc))
        acc = acc + jnp.where(ok[None, :, None], xp[:, j:j + t], 0.0) * w[j]
    return acc * jax.nn.sigmoid(acc)


def _deltanet(qkv, ab, conv_w, a_log, dt_bias, lc, seg):
    b, t, _ = qkv.shape
    H, C = DN_HEADS, DN_CHUNK
    act = _conv_silu(qkv, conv_w, lc)
    a = ab[..., :2 * H].reshape(b, t, 2, H)
    bt = ab[..., 2 * H:4 * H].reshape(b, t, 2, H)
    g = jnp.transpose(-jnp.exp(a_log) * jax.nn.softplus(a + dt_bias), (2, 0, 3, 1))
    beta = jnp.transpose(jax.nn.sigmoid(bt), (2, 0, 3, 1))
    gch = g.reshape(2, b, H, t // C, C)
    pre = jnp.cumsum(gch, axis=-1)
    gc = jnp.stack([pre[0], jnp.sum(gch[1], -1, keepdims=True) - pre[1] + gch[1]])
    bc = beta.reshape(2, b, H, t // C, C)
    rows = jnp.pad(jnp.stack([gc, bc], axis=-2), ((0, 0),) * 4 + ((0, 6), (0, 0)))
    aux = jnp.pad(jnp.stack([gc.reshape(2, b, H, t), beta], axis=-1), ((0, 0),) * 4 + ((0, 6),))
    nseg, cs = t // seg, lc // seg

    def visit_order(a_, per_seg):
        shp = a_.shape
        a_ = a_.reshape(shp[:2] + (nseg, per_seg) + shp[3:])
        a_ = jnp.concatenate([jnp.flip(a_[:, :, :cs], 2), jnp.flip(a_[:, :, cs:], 2)], axis=2)
        return a_.reshape(shp)

    rows = jnp.stack([rows[0], visit_order(rows[1], seg // C)])
    aux = jnp.stack([aux[0], visit_order(aux[1], seg)])
    return deltanet_scan(act, aux, rows, heads=H, dk=DN_DK, dv=DN_DV, chunk=C, seg=seg, ctx_segs=cs)


def _pad_cols(w, width):
    return jnp.pad(w, ((0, 0), (0, width - w.shape[1])))


def kernel(x, c, ctx, c_ctx, ada_w, ada_b, ln1_g, ln1_b, ln2_g, ln2_b, ar_w_in, mla_q_norm, mla_w_uq,
           mla_kv_norm, mla_w_ukv, ret_gn_g, ar_w_out, dn_w_in, dn_conv, dn_a_log, dn_dt_bias, dn_norm_g,
           dn_w_out, peer_w_q, peer_k1, peer_k2, peer_u, peer_v):
    bsz, L, D = x.shape
    lc = ctx.shape[1]
    T = lc + L
    tm = 256
    assert lc % tm == 0 and L % tm == 0
    cb = lc // tm
    seg = tm

    X = jnp.concatenate([ctx, x], axis=1)
    cc = jnp.zeros((8, D), F32).at[:bsz].set(c).at[bsz].set(c_ctx)
    mod_all = ada_all(cc, ada_w, ada_b)

    cos_m, sin_m = _rope_tables(L, MLA_ROPE, lc)
    cos_r, sin_r = _rope_tables(L, RET_DK, lc)

    for l in range(DEPTH):
        j = l // 2
        mod = mod_all[l]
        ml = mod[:bsz].reshape(bsz, N_MOD, D)
        mc = jnp.broadcast_to(mod[bsz].reshape(1, N_MOD, D), (bsz, N_MOD, D))
        msel = jnp.stack([mc, ml], axis=1)
        mvec = [msel[:, :, i][:, :, None, :] for i in range(N_MOD)]
        sh1, sc1, g1, sh2, sc2, g2 = mvec

        if l % 2 == 0:
            w_in = ar_w_in[j]
            w_pad = jnp.concatenate(
                [w_in[:, :416], jnp.zeros((D, 96), F32), w_in[:, 416:]], axis=1).astype(BF16)
            splits = [(0, 256), (256, 384), (384, 512), (512, 1024), (1024, 1536), (1536, 2048), (2048, 2560)]
            cq, ckv, krp, rq, rk, rv, rg = proj(X, w_pad, splits, mode="mod", shift=sh1, scale=sc1,
                                                ctx_blocks=cb, tm=tm, name="ar_in")
            (qf,) = proj(cq, mla_w_uq[j].astype(BF16), [(0, MLA_HEADS * (MLA_NOPE + MLA_ROPE))], mode="rms",
                         gain=mla_q_norm[j], tm=tm, name="mla_uq")
            (kvf,) = proj(ckv, mla_w_ukv[j].astype(BF16), [(0, MLA_HEADS * (MLA_NOPE + MLA_V))], mode="rms",
                          gain=mla_kv_norm[j], tm=tm, name="mla_ukv")
            qf = qf.reshape(bsz, T, MLA_HEADS, MLA_NOPE + MLA_ROPE)
            kvf = kvf.reshape(bsz, T, MLA_HEADS, MLA_NOPE + MLA_V)
            qn, qr = qf[..., :MLA_NOPE], qf[..., MLA_NOPE:]
            kn, vv = kvf[..., :MLA_NOPE], kvf[..., MLA_NOPE:]
            qr = _apply_rope(qr, cos_m[:, None], sin_m[:, None])
            kr = _apply_rope(krp[..., :MLA_ROPE], cos_m, sin_m)
            qh = jnp.transpose(jnp.concatenate([qn, qr], -1) * MLA_SCALE, (0, 2, 1, 3)).astype(BF16)
            kh = jnp.transpose(jnp.concatenate(
                [kn, jnp.broadcast_to(kr[:, :, None, :], (bsz, T, MLA_HEADS, MLA_ROPE))], -1), (0, 2, 1, 3)).astype(BF16)
            vh = jnp.transpose(vv, (0, 2, 1, 3)).astype(BF16)
            tk = next(c for c in (1408, 768, lc) if T % c == 0)
            o_l = attention(qh[:, :, lc:], kh, vh, tq=512, tk=tk)
            o_c = attention(qh[:, :, :lc], kh[:, :, :lc], vh[:, :, :lc], tq=lc, tk=lc)
            mla = jnp.transpose(jnp.concatenate([o_c, o_l], axis=2), (0, 2, 1, 3)).reshape(bsz, T, MLA_HEADS * MLA_V)
            ro = _retention(rq, rk, rv, cos_r, sin_r, lc, seg)
            ret = _group_norm_heads(ro, ret_gn_g[j]) * jax.nn.silu(rg)
            mix = jnp.concatenate([mla, ret], axis=-1)
            (y,) = proj(mix, ar_w_out[j].astype(BF16), [(0, D)], tm=tm, name="ar_out")
        else:
            w_pad = _pad_cols(dn_w_in[j], 4224).astype(BF16)
            splits = [(0, 3072), (3072, 4096), (4096, 4224)]
            qkv, gate, ab = proj(X, w_pad, splits, mode="mod", shift=sh1, scale=sc1, ctx_blocks=cb, tm=tm,
                                 name="dn_in")
            o_f, o_b = _deltanet(qkv, ab, dn_conv[j], dn_a_log[j], dn_dt_bias[j], lc, seg)
            (y,) = proj(o_f, dn_w_out[j].astype(BF16), [(0, D)], mode="gated_rms", other=o_b, gate=gate,
                        gain=dn_norm_g[j], group=DN_DV, tm=tm, name="dn_out")

        X = resid_ln(X, y, g1, ln1_g[l], ln1_b[l], ctx_blocks=cb, tm=tm)
        f = peer(X, sh2, sc2, peer_w_q[l].T.astype(BF16), peer_k1[l].astype(BF16), peer_k2[l].astype(BF16),
                 peer_u[l].astype(BF16), peer_v[l].T.astype(BF16), ctx_blocks=cb)
        X = resid_ln(X, f, g2, ln2_g[l], ln2_b[l], ctx_blocks=cb, tm=tm)

    return X[:, lc:]
```

```python
import functools
import math

import numpy as np
import jax
import jax.numpy as jnp
from jax import lax
from jax.experimental import pallas as pl
from jax.experimental.pallas import tpu as pltpu

F32 = jnp.float32
BF16 = jnp.bfloat16

DEPTH = 4
GRID_W = 64
ROPE_BASE = 10000.0
N_MOD = 6

MLA_HEADS = 8
MLA_Q_RANK = 256
MLA_KV_RANK = 128
MLA_NOPE = 64
MLA_ROPE = 32
MLA_V = 64
MLA_SCALE = (MLA_NOPE + MLA_ROPE) ** -0.5

RET_HEADS = 8
RET_DK = 64
RET_DV = 64
RET_CHUNK = 128

DN_HEADS = 8
DN_DK = 128
DN_DV = 128
DN_CONV = 5
DN_CHUNK = 64

PEER_HEADS = 8
PEER_KEYS = 128
PEER_QDIM = 256
PEER_TOPK = 16

DEEPNORM_ALPHA = (2 * DEPTH) ** 0.25

LANE = 128
VMEM_LIMIT = 56 * 1024 * 1024

NT = (((1,), (1,)), ((), ()))
TN = (((0,), (0,)), ((), ()))


def _cparams(sem):
    return pltpu.CompilerParams(dimension_semantics=sem, vmem_limit_bytes=VMEM_LIMIT)


def _ada_kernel(c_ref, w_ref, b_ref, o_ref):
    c = c_ref[...]
    a = (c * jax.nn.sigmoid(c)).astype(BF16)
    o_ref[...] = jnp.dot(a, w_ref[...].astype(BF16), preferred_element_type=F32) + b_ref[...]


def ada_all(cc, ada_w, ada_b, tn=1024):
    depth, d, n = ada_w.shape
    m = cc.shape[0]
    return pl.pallas_call(
        _ada_kernel,
        grid=(depth, n // tn),
        in_specs=[
            pl.BlockSpec((m, d), lambda l, j: (0, 0)),
            pl.BlockSpec((None, d, tn), lambda l, j: (l, 0, j)),
            pl.BlockSpec((None, 1, tn), lambda l, j: (l, 0, j)),
        ],
        out_specs=pl.BlockSpec((None, m, tn), lambda l, j: (l, 0, j)),
        out_shape=jax.ShapeDtypeStruct((depth, m, n), F32),
        compiler_params=_cparams(("arbitrary", "arbitrary")),
        name="ada",
    )(cc, ada_w, ada_b.reshape(depth, 1, n))


def _resid_ln(x, y, gate, g, b, eps):
    v = DEEPNORM_ALPHA * x + gate * y
    vc = v - jnp.mean(v, axis=-1, keepdims=True)
    var = jnp.mean(vc * vc, axis=-1, keepdims=True)
    return vc * lax.rsqrt(var + eps) * g + b


def _proj_kernel(*refs, mode, splits, eps, group, resid, ln_eps):
    n_in = {"mod": 4, "rms": 3, "gated_rms": 5, "none": 2}[mode]
    ins, rest = refs[:n_in], refs[n_in:]
    w_ref = ins[-1]
    if mode == "mod":
        x_ref, sh_ref, sc_ref = ins[:3]
        x = x_ref[...] * (1.0 + sc_ref[...]) + sh_ref[...]
    elif mode == "rms":
        x_ref, g_ref = ins[:2]
        x = x_ref[...]
        x = x * lax.rsqrt(jnp.mean(x * x, axis=-1, keepdims=True) + eps) * g_ref[...]
    elif mode == "gated_rms":
        a_ref, b_ref, gate_ref, g_ref = ins[:4]
        o = a_ref[...] + b_ref[...]
        gate = gate_ref[...]
        parts = []
        for h in range(o.shape[1] // group):
            oh = o[:, h * group:(h + 1) * group]
            parts.append(oh * lax.rsqrt(jnp.mean(oh * oh, axis=-1, keepdims=True) + eps))
        x = jnp.concatenate(parts, axis=1) * g_ref[...] * (gate * jax.nn.sigmoid(gate))
    else:
        x = ins[0][...]
    z = jnp.dot(x.astype(BF16), w_ref[...], preferred_element_type=F32)
    if resid:
        xs_ref, gt_ref, lg_ref, lb_ref, o_ref = rest
        o_ref[...] = _resid_ln(xs_ref[...], z, gt_ref[...], lg_ref[...], lb_ref[...], ln_eps)
        return
    for o_ref, (a, b) in zip(rest, splits):
        o_ref[...] = z[:, a:b].astype(o_ref.dtype)


def proj(x, w, splits, *, mode="none", shift=None, scale=None, gain=None, other=None, gate=None, group=LANE,
         resid=None, ctx_blocks=1, tm=256, eps=1e-6, ln_eps=1e-5, name="proj"):
    bsz, t, k = x.shape
    n = w.shape[1]
    row = lambda b, j: (b, j, 0)
    sel = lambda b, j: (b, jnp.minimum(j // ctx_blocks, 1), 0, 0)
    vec = lambda b, j: (0, 0)
    in_specs = [pl.BlockSpec((None, tm, k), row)]
    args = [x]
    if mode == "mod":
        in_specs += [pl.BlockSpec((None, None, 1, k), sel)] * 2
        args += [shift, scale]
    elif mode == "rms":
        in_specs += [pl.BlockSpec((1, k), vec)]
        args += [gain.reshape(1, k)]
    elif mode == "gated_rms":
        in_specs += [pl.BlockSpec((None, tm, k), row)] * 2 + [pl.BlockSpec((1, k), vec)]
        args += [other, gate, jnp.tile(gain, k // group).reshape(1, k)]
    in_specs += [pl.BlockSpec((k, n), vec)]
    args += [w]
    if resid is not None:
        stream, gate_sel, ln_g, ln_b = resid
        assert list(splits) == [(0, n)] and stream.shape[-1] == n
        in_specs += [pl.BlockSpec((None, tm, n), row), pl.BlockSpec((None, None, 1, n), sel),
                     pl.BlockSpec((1, n), vec), pl.BlockSpec((1, n), vec)]
        args += [stream, gate_sel, ln_g.reshape(1, n), ln_b.reshape(1, n)]
    return pl.pallas_call(
        functools.partial(_proj_kernel, mode=mode, splits=tuple(splits), eps=eps, group=group,
                          resid=resid is not None, ln_eps=ln_eps),
        grid=(bsz, t // tm),
        in_specs=in_specs,
        out_specs=[pl.BlockSpec((None, tm, b - a), row) for a, b in splits],
        out_shape=[jax.ShapeDtypeStruct((bsz, t, b - a), F32) for a, b in splits],
        compiler_params=_cparams(("parallel", "parallel")),
        name=name,
    )(*args)


def _attn_kernel(q_ref, k_ref, v_ref, o_ref, *, tk, nk, hb):
    tq = q_ref.shape[1]
    dv = v_ref.shape[-1]
    qs = [q_ref[h] for h in range(hb)]

    def body(i, carry):
        start = pl.multiple_of(i * tk, tk)
        ss = [lax.dot_general(qs[h], k_ref[h, pl.ds(start, tk), :], NT, preferred_element_type=F32)
              for h in range(hb)]
        ps, new = [], []
        for h in range(hb):
            m, l, acc = carry[h]
            m_new = jnp.maximum(m, jnp.max(ss[h], axis=1, keepdims=True))
            p = jnp.exp(ss[h] - m_new)
            alpha = jnp.exp(m - m_new)
            ps.append(p.astype(BF16))
            new.append((m_new, alpha * l + jnp.sum(p, axis=1, keepdims=True), alpha * acc))
        out = []
        for h in range(hb):
            m_new, l, acc = new[h]
            out.append((m_new, l, acc + jnp.dot(ps[h], v_ref[h, pl.ds(start, tk), :], preferred_element_type=F32)))
        return tuple(out)

    init = tuple((jnp.full((tq, 1), -jnp.inf, F32), jnp.zeros((tq, 1), F32), jnp.zeros((tq, dv), F32))
                 for _ in range(hb))
    fin = lax.fori_loop(0, nk, body, init)
    for h in range(hb):
        _, l, acc = fin[h]
        o_ref[h] = acc / l


def attention(q, k, v, *, tq, tk, hb=2):
    bsz, h, lq, dq = q.shape
    lk, dv = v.shape[2], v.shape[3]
    assert lq % tq == 0 and lk % tk == 0 and h % hb == 0
    return pl.pallas_call(
        functools.partial(_attn_kernel, tk=tk, nk=lk // tk, hb=hb),
        grid=(bsz, h // hb, lq // tq),
        in_specs=[pl.BlockSpec((None, hb, tq, dq), lambda b, hh, i: (b, hh, i, 0)),
                  pl.BlockSpec((None, hb, lk, dq), lambda b, hh, i: (b, hh, 0, 0)),
                  pl.BlockSpec((None, hb, lk, dv), lambda b, hh, i: (b, hh, 0, 0))],
        out_specs=pl.BlockSpec((None, hb, tq, dv), lambda b, hh, i: (b, hh, i, 0)),
        out_shape=jax.ShapeDtypeStruct((bsz, h, lq, dv), F32),
        compiler_params=_cparams(("parallel", "parallel", "parallel")),
        name="mla_attn",
    )(q, k, v)


def _top_values(s, k):
    vals = []
    cur = s
    for i in range(k):
        m = jnp.max(cur, axis=0, keepdims=True)
        vals.append(m)
        if i + 1 < k:
            cur = jnp.where(cur >= m, -jnp.inf, cur)
    return vals


def _gelu_exact(x):
    return 0.5 * x * (1.0 + lax.erf(x * (2.0 ** -0.5)))


def _peer_prepare(x_ref, sh_ref, sc_ref, wqt_ref, k1_ref, k2_ref, xb_ref, thr_ref, s2_ref, e1_ref, e2_ref):
    half = PEER_QDIM // 2
    K = PEER_TOPK
    H = PEER_HEADS
    x = x_ref[...] * (1.0 + sc_ref[...]) + sh_ref[...]
    xb = x.astype(BF16)
    xb_ref[...] = xb
    qt = lax.dot_general(wqt_ref[...], xb, NT, preferred_element_type=F32)
    v1s, v2s = [], []
    for h in range(H):
        q1 = qt[h * PEER_QDIM: h * PEER_QDIM + half].astype(BF16)
        q2 = qt[h * PEER_QDIM + half: (h + 1) * PEER_QDIM].astype(BF16)
        s1 = jnp.dot(k1_ref[h], q1, preferred_element_type=F32)
        s2 = jnp.dot(k2_ref[h], q2, preferred_element_type=F32)
        thr_ref[h] = s1
        s2_ref[h] = s2
        v1s.append(_top_values(s1, K + 1))
        v2s.append(_top_values(s2, K + 1))
    v1 = [jnp.concatenate([v1s[h][a] for h in range(H)], axis=0) for a in range(K + 1)]
    v2 = [jnp.concatenate([v2s[h][b] for h in range(H)], axis=0) for b in range(K + 1)]
    cands = [v1[a] + v2[b] for a in range(K + 1) for b in range(K + 1) if (a + 1) * (b + 1) <= K + 1]
    cur = list(cands)
    for i in range(K + 1):
        m = functools.reduce(jnp.maximum, cur)
        if i == K - 1:
            kth = m
        if i < K:
            cur = [jnp.where(c >= m, -jnp.inf, c) for c in cur]
    tau = 0.5 * (kth + m)
    top = cands[0]
    z = functools.reduce(jnp.add, [jnp.where(c >= tau, jnp.exp(c - top), 0.0) for c in cands])
    for h in range(H):
        s1 = thr_ref[h]
        thr_ref[h] = tau[h:h + 1] - s1
        e1_ref[h] = jnp.exp(s1 - v1s[h][0]) / z[h:h + 1]
        e2_ref[h] = jnp.exp(s2_ref[h] - v2s[h][0])


def _peer_weights(blk, rows, row_lo, row_hi, thr_ref, s2_ref, e1_ref, e2_ref, ht_ref, a_ref):
    tb = ht_ref.shape[1]
    r0 = blk * rows
    for r in range(row_lo, row_hi):
        rs = slice(r * PEER_KEYS, (r + 1) * PEER_KEYS)
        thr_rows = [thr_ref[h, pl.ds(r0 + r, 1), :] for h in range(PEER_HEADS)]
        e1_rows = [e1_ref[h, pl.ds(r0 + r, 1), :] for h in range(PEER_HEADS)]
        for lt in range(tb // LANE):
            ls = slice(lt * LANE, (lt + 1) * LANE)
            w = None
            for h in range(PEER_HEADS):
                term = jnp.where(s2_ref[h, :, ls] >= thr_rows[h][:, ls], e1_rows[h][:, ls] * e2_ref[h, :, ls], 0.0)
                w = term if w is None else w + term
            a_ref[rs, ls] = (w * _gelu_exact(ht_ref[rs, ls])).astype(BF16)


def _peer_kernel(x_ref, sh_ref, sc_ref, gt_ref, lg_ref, lb_ref, wqt_ref, k1_ref, k2_ref,
                 u0_ref, ua_ref, ub_ref, vta_ref, vtb_ref, o_ref,
                 xb_ref, thr_ref, s2_ref, e1_ref, e2_ref, hte_ref, hto_ref, ae_ref, ao_ref, acc_ref, *, rows, ln_eps):
    g = pl.program_id(2)
    last = pl.num_programs(2) - 1
    p = g % 2
    q = 1 - p
    tabs = (thr_ref, s2_ref, e1_ref, e2_ref)

    @pl.when(g == 0)
    def _first():
        _peer_prepare(x_ref, sh_ref, sc_ref, wqt_ref, k1_ref, k2_ref, xb_ref, *tabs)
        acc_ref[...] = jnp.zeros_like(acc_ref)
        ao_ref[1] = jnp.zeros(ao_ref.shape[1:], ao_ref.dtype)
        hte_ref[0] = lax.dot_general(u0_ref[...], xb_ref[...], NT, preferred_element_type=F32)

    @pl.when(g < last)
    def _even():
        hto_ref[...] = lax.dot_general(ua_ref[...], xb_ref[...], NT, preferred_element_type=F32)
        acc_ref[...] += jnp.dot(vta_ref[...], ao_ref[q], preferred_element_type=F32)
        _peer_weights(2 * g, rows, 0, rows, *tabs, hte_ref.at[p], ae_ref)

    @pl.when(g < last)
    def _odd():
        hte_ref[q] = lax.dot_general(ub_ref[...], xb_ref[...], NT, preferred_element_type=F32)
        acc_ref[...] += jnp.dot(vtb_ref[...], ae_ref[...], preferred_element_type=F32)
        _peer_weights(2 * g + 1, rows, 0, rows, *tabs, hto_ref, ao_ref.at[p])

    @pl.when(g == last)
    def _finish():
        acc = acc_ref[...] + jnp.dot(vta_ref[...], ao_ref[q], preferred_element_type=F32)
        o_ref[...] = _resid_ln(x_ref[...], acc.T, gt_ref[...], lg_ref[...], lb_ref[...], ln_eps)


def peer(x, shift, scale, gate, ln_g, ln_b, wqt, k1, k2, u, vt, *, ctx_blocks, tb=256, eb=1024, ln_eps=1e-5):
    bsz, t, d = x.shape
    n = u.shape[0]
    ne = n // eb
    assert ne % 2 == 0
    rows = eb // PEER_KEYS
    row = lambda b, j, g: (b, j, 0)
    sel = lambda b, j, g: (b, jnp.minimum(j // ctx_blocks, 1), 0, 0)
    const2 = lambda b, j, g: (0, 0)
    const3 = lambda b, j, g: (0, 0, 0)
    tab = pltpu.VMEM((PEER_HEADS, PEER_KEYS, tb), F32)
    ublk = lambda f: pl.BlockSpec((eb, d), lambda b, j, g: (f(g), 0))
    vblk = lambda f: pl.BlockSpec((d, eb), lambda b, j, g: (0, f(g)))
    return pl.pallas_call(
        functools.partial(_peer_kernel, rows=rows, ln_eps=ln_eps),
        grid=(bsz, t // tb, ne // 2 + 1),
        in_specs=[pl.BlockSpec((None, tb, d), row),
                  pl.BlockSpec((None, None, 1, d), sel), pl.BlockSpec((None, None, 1, d), sel),
                  pl.BlockSpec((None, None, 1, d), sel), pl.BlockSpec((1, d), const2), pl.BlockSpec((1, d), const2),
                  pl.BlockSpec(wqt.shape, const2),
                  pl.BlockSpec(k1.shape, const3), pl.BlockSpec(k2.shape, const3),
                  ublk(lambda g: 0),
                  ublk(lambda g: jnp.minimum(2 * g + 1, ne - 1)),
                  ublk(lambda g: jnp.minimum(2 * g + 2, ne - 1)),
                  vblk(lambda g: jnp.maximum(2 * g - 1, 0)),
                  vblk(lambda g: jnp.minimum(2 * g, ne - 1))],
        out_specs=pl.BlockSpec((None, tb, d), row),
        out_shape=jax.ShapeDtypeStruct((bsz, t, d), F32),
        scratch_shapes=[pltpu.VMEM((tb, d), BF16), tab, tab, tab, tab,
                        pltpu.VMEM((2, eb, tb), F32), pltpu.VMEM((eb, tb), F32),
                        pltpu.VMEM((eb, tb), BF16), pltpu.VMEM((2, eb, tb), BF16),
                        pltpu.VMEM((d, tb), F32)],
        compiler_params=_cparams(("parallel", "parallel", "arbitrary")),
        name="peer",
    )(x, shift, scale, gate, ln_g.reshape(1, d), ln_b.reshape(1, d), wqt, k1, k2, u, u, u, vt, vt)


def _bwd_seg(s, cs, nseg):
    return jnp.where(s < cs, cs - 1 - s, nseg - 1 - (s - cs))


def _ret_kernel(qf_ref, kf_ref, vf_ref, qb_ref, kb_ref, vb_ref, d_ref, qdec_ref, kdec_ref, cdec_ref,
                of_ref, ob_ref, s_ref, *, chunk, nchunk):
    @pl.when(pl.program_id(2) == 0)
    def _init():
        s_ref[...] = jnp.zeros_like(s_ref)

    cdec = cdec_ref[...]
    ins = ((qf_ref, kf_ref, vf_ref, of_ref), (qb_ref, kb_ref, vb_ref, ob_ref))

    def step(i, carry):
        new = []
        for d in range(2):
            q_ref, k_ref, v_ref, o_ref = ins[d]
            c = i if d == 0 else nchunk - 1 - i
            s = carry[d]
            sl = pl.ds(pl.multiple_of(c * chunk, chunk), chunk)
            q = q_ref[sl, :]
            k = k_ref[sl, :]
            v = v_ref[sl, :]
            scores = lax.dot_general(q, k, NT, preferred_element_type=F32) * d_ref[d]
            intra = jnp.dot(scores.astype(BF16), v, preferred_element_type=F32)
            qd = (q.astype(F32) * qdec_ref[d]).astype(BF16)
            inter = jnp.dot(qd, s.astype(BF16), preferred_element_type=F32)
            o_ref[sl, :] = intra + inter
            kd = (k.astype(F32) * kdec_ref[d]).astype(BF16)
            new.append(s * cdec + lax.dot_general(kd, v, TN, preferred_element_type=F32))
        return tuple(new)

    s0, s1 = lax.fori_loop(0, nchunk, step, (s_ref[0], s_ref[1]))
    s_ref[0] = s0
    s_ref[1] = s1


def retention_scan(q, k, v, dmat, qdec, kdec, cdec, *, chunk, seg, ctx_segs):
    bsz, h, t, dk = q.shape
    dv = v.shape[-1]
    assert t % seg == 0 and seg % chunk == 0
    nseg = t // seg
    fwd = lambda d_: pl.BlockSpec((None, None, seg, d_), lambda b, hh, s: (b, hh, s, 0))
    bwd = lambda d_: pl.BlockSpec((None, None, seg, d_), lambda b, hh, s: (b, hh, _bwd_seg(s, ctx_segs, nseg), 0))
    per_head = lambda *shape: pl.BlockSpec((2, None) + shape, lambda b, hh, s: (0, hh, 0, 0))
    return pl.pallas_call(
        functools.partial(_ret_kernel, chunk=chunk, nchunk=seg // chunk),
        grid=(bsz, h, nseg),
        in_specs=[fwd(dk), fwd(dk), fwd(dv), bwd(dk), bwd(dk), bwd(dv),
                  per_head(chunk, chunk), per_head(chunk, dk), per_head(chunk, dk),
                  pl.BlockSpec((None, dk, dv), lambda b, hh, s: (hh, 0, 0))],
        out_specs=[fwd(dv), bwd(dv)],
        out_shape=[jax.ShapeDtypeStruct((bsz, h, t, dv), F32)] * 2,
        scratch_shapes=[pltpu.VMEM((2, dk, dv), F32)],
        compiler_params=_cparams(("parallel", "parallel", "arbitrary")),
        name="retention",
    )(q, k, v, q, k, v, dmat, qdec, kdec, cdec)


def _dn_kernel(qf_ref, kf_ref, vf_ref, qb_ref, kb_ref, vb_ref, aux_ref, rows_ref, of_ref, ob_ref,
               s_ref, m_s, b_s, qp_s, cd_s, sall_s, *, chunk, nchunk, hb, eps):
    C = chunk

    @pl.when(pl.program_id(2) == 0)
    def _init():
        s_ref[...] = jnp.zeros_like(s_ref)

    ri = lax.broadcasted_iota(jnp.int32, (C, C), 0)
    ci = lax.broadcasted_iota(jnp.int32, (C, C), 1)
    eye = (ri == ci).astype(F32)
    bmm = lambda a, b: jnp.einsum('nij,njk->nik', a.astype(BF16), b.astype(BF16), preferred_element_type=F32)
    bmm_nt = lambda a, b: jnp.einsum('nid,njd->nij', a.astype(BF16), b.astype(BF16),
                                     preferred_element_type=F32)
    bmm_tn = lambda a, b: jnp.einsum('nci,ncj->nij', a.astype(BF16), b.astype(BF16),
                                     preferred_element_type=F32)
    ins = ((qf_ref, kf_ref, vf_ref, of_ref), (qb_ref, kb_ref, vb_ref, ob_ref))
    dk = qf_ref.shape[-1] // hb
    dv = vf_ref.shape[-1] // hb
    chains = [(d, j) for d in range(2) for j in range(hb)]
    nb = len(chains) * nchunk

    def gather(fn):
        return jnp.concatenate([fn(d, j) for d, j in chains], axis=0)

    def l2n(ref, j, scale):
        t = ref[:, j * dk:(j + 1) * dk]
        return (t * (lax.rsqrt(jnp.sum(t * t, axis=-1, keepdims=True) + eps) * scale)).reshape(nchunk, C, dk)

    q = gather(lambda d, j: l2n(ins[d][0], j, dk ** -0.5))
    k = gather(lambda d, j: l2n(ins[d][1], j, 1.0))
    v = gather(lambda d, j: ins[d][2][:, j * dv:(j + 1) * dv].reshape(nchunk, C, dv))
    aux = gather(lambda d, j: aux_ref[d, j].reshape(nchunk, C, aux_ref.shape[-1]))
    rws = gather(lambda d, j: rows_ref[d, j])
    gcol = aux[:, :, 0:1]
    bcol = aux[:, :, 1:2]
    grow = rws[:, 0:1, :]
    brow = rws[:, 1:2, :]
    is_fwd = lax.broadcasted_iota(jnp.int32, (nb, 1, 1), 0) < hb * nchunk
    signed = (ri - ci) * jnp.where(is_fwd, 1, -1)
    incl = signed >= 0
    strict = signed > 0
    kk = bmm_nt(k, k)
    qk = bmm_nt(q, k)
    decay = jnp.where(incl, jnp.exp(jnp.where(incl, gcol - grow, 0.0)), 0.0)
    x = jnp.where(strict, -(kk * decay * bcol), 0.0)
    x2 = bmm(x, x)
    x4 = bmm(x2, x2)
    x8 = bmm(x4, x4)
    x16 = bmm(x8, x8)
    x32 = bmm(x16, x16)
    p1 = eye + x + x2 + bmm(x, x2)
    p2 = eye + x4 + x8 + bmm(x4, x8)
    p3 = eye + x16 + x32 + bmm(x16, x32)
    tinv = bmm(bmm(p1, p2), p3)
    egrow = jnp.exp(grow)
    u = bmm(tinv * brow, v)
    w = bmm(tinv * (brow * egrow), k)
    attn = qk * decay
    gtot = jnp.where(is_fwd, grow[:, :, C - 1:C], grow[:, :, 0:1])
    qd = q * jnp.exp(gcol)
    kd = k * jnp.exp(gtot - gcol)
    per_chain = lambda t: t.reshape((2, hb, nchunk) + t.shape[1:])
    m_s[...] = per_chain(bmm_tn(kd, w).astype(BF16))
    b_s[...] = per_chain(bmm_tn(kd, u))
    qp_s[...] = per_chain((qd - bmm(attn, w)).astype(BF16))
    cd_s[...] = per_chain(jnp.broadcast_to(jnp.exp(gtot), (nb, 1, dv)))
    o_local = bmm(attn, u)

    def step(i, carry):
        new = []
        for (d, j), s in zip(chains, carry):
            c = i if d == 0 else nchunk - 1 - i
            sb = s.astype(BF16)
            sall_s[d, j, c] = sb
            new.append(s * cd_s[d, j, c] - jnp.dot(m_s[d, j, c], sb, preferred_element_type=F32) + b_s[d, j, c])
        return tuple(new)

    final = lax.fori_loop(0, nchunk, step, tuple(s_ref[d, j] for d, j in chains))
    for (d, j), s in zip(chains, final):
        s_ref[d, j] = s

    o_all = o_local + bmm(qp_s[...].reshape(nb, C, dk), sall_s[...].reshape(nb, dk, dv))
    for idx, (d, j) in enumerate(chains):
        ins[d][3][:, j * dv:(j + 1) * dv] = o_all[idx * nchunk:(idx + 1) * nchunk].reshape(nchunk * C, dv)


def deltanet_scan(qkv, aux, rows, *, heads, dk, dv, chunk, seg, ctx_segs, hb=4, eps=1e-6):
    bsz, t, _ = qkv.shape
    assert dk == dv == LANE and t % seg == 0 and seg % chunk == 0 and heads % hb == 0
    nseg = t // seg
    nchunk = seg // chunk
    hg = heads // hb
    fwd = lambda off: pl.BlockSpec((None, seg, hb * dk), lambda b, hh, s: (b, s, off + hh))
    bwd = lambda off: pl.BlockSpec((None, seg, hb * dk), lambda b, hh, s: (b, _bwd_seg(s, ctx_segs, nseg), off + hh))
    return pl.pallas_call(
        functools.partial(_dn_kernel, chunk=chunk, nchunk=nchunk, hb=hb, eps=eps),
        grid=(bsz, hg, nseg),
        in_specs=[fwd(0), fwd(hg), fwd(2 * hg), bwd(0), bwd(hg), bwd(2 * hg),
                  pl.BlockSpec((2, None, hb, seg, aux.shape[-1]), lambda b, hh, s: (0, b, hh, s, 0)),
                  pl.BlockSpec((2, None, hb, nchunk, 8, chunk), lambda b, hh, s: (0, b, hh, s, 0, 0))],
        out_specs=[fwd(0), bwd(0)],
        out_shape=[jax.ShapeDtypeStruct((bsz, t, heads * dv), F32)] * 2,
        scratch_shapes=[pltpu.VMEM((2, hb, dk, dv), F32),
                        pltpu.VMEM((2, hb, nchunk, dk, dv), BF16),
                        pltpu.VMEM((2, hb, nchunk, dk, dv), F32),
                        pltpu.VMEM((2, hb, nchunk, chunk, dk), BF16),
                        pltpu.VMEM((2, hb, nchunk, 1, dv), F32),
                        pltpu.VMEM((2, hb, nchunk, dk, dv), BF16)],
        compiler_params=_cparams(("parallel", "parallel", "arbitrary")),
        name="deltanet",
    )(qkv, qkv, qkv, qkv, qkv, qkv, aux, rows)


def _rope_tables(length, dim, ctx_len):
    rows = length // GRID_W
    n_freq = dim // 4
    inv = ROPE_BASE ** (-jnp.arange(n_freq, dtype=F32) / n_freq)
    r = jnp.repeat(jnp.arange(rows, dtype=F32), GRID_W)
    c = jnp.tile(jnp.arange(GRID_W, dtype=F32), rows)
    ang = jnp.concatenate([r[:, None] * inv, c[:, None] * inv], axis=-1)
    cos = jnp.concatenate([jnp.ones((ctx_len, dim // 2), F32), jnp.cos(ang)], axis=0)
    sin = jnp.concatenate([jnp.zeros((ctx_len, dim // 2), F32), jnp.sin(ang)], axis=0)
    return cos, sin


def _apply_rope(x, cos, sin):
    x1, x2 = jnp.split(x, 2, axis=-1)
    return jnp.concatenate([x1 * cos - x2 * sin, x2 * cos + x1 * sin], axis=-1)


def _retention(rq, rk, rv, cos, sin, lc, seg):
    b, t, _ = rq.shape
    q = _apply_rope(rq.reshape(b, t, RET_HEADS, RET_DK), cos[:, None], sin[:, None])
    k = _apply_rope(rk.reshape(b, t, RET_HEADS, RET_DK), cos[:, None], sin[:, None]) * RET_DK ** -0.5
    v = rv.reshape(b, t, RET_HEADS, RET_DV)
    tr = lambda a: jnp.transpose(a, (0, 2, 1, 3)).astype(BF16)
    lg = jnp.log1p(-jnp.exp2(-5.0 - jnp.arange(RET_HEADS, dtype=F32)))[:, None, None]
    C = RET_CHUNK
    pos = jnp.arange(C, dtype=F32)
    dist = pos[:, None] - pos[None, :]
    dm = lambda dd, mask: jnp.where(mask, jnp.exp(jnp.where(mask, dd, 0.0) * lg), 0.0)
    dmat = jnp.stack([dm(dist, dist >= 0), dm(-dist, dist < 0)])
    col = lambda e: jnp.broadcast_to(jnp.exp(e * lg[:, :, 0])[..., None], (RET_HEADS, C, RET_DK))
    qdec = jnp.stack([col(pos + 1.0), col(C - pos)])
    kdec = jnp.stack([col(C - 1.0 - pos), col(pos)])
    cdec = jnp.broadcast_to(jnp.exp(C * lg), (RET_HEADS, RET_DK, RET_DV))
    o_f, o_b = retention_scan(tr(q), tr(k), tr(v), dmat, qdec, kdec, cdec, chunk=C, seg=seg, ctx_segs=lc // seg)
    return o_f + o_b


def _group_norm_heads(o, g, eps=1e-5):
    oc = o - jnp.mean(o, -1, keepdims=True)
    y = oc * lax.rsqrt(jnp.mean(oc * oc, -1, keepdims=True) + eps)
    b, h, L, dv = o.shape
    return jnp.transpose(y, (0, 2, 1, 3)).reshape(b, L, h * dv) * g


def _conv_silu(x, w, lc):
    b, t, ch = x.shape
    K = w.shape[0]
    half = (K - 1) // 2
    xp = jnp.pad(x, ((0, 0), (half, K // 2), (0, 0)))
    pos = jnp.arange(t)
    acc = 0.0
    for j in range(K):
        src = pos + (j - half)
        ok = (src >= 0) & (src < t) & ((pos < lc) == (src < lc))
        acc = acc + jnp.where(ok[None, :, None], xp[:, j:j + t], 0.0) * w[j]
    return acc * jax.nn.sigmoid(acc)


def _deltanet(qkv, ab, conv_w, a_log, dt_bias, lc, seg):
    b, t, _ = qkv.shape
    H, C = DN_HEADS, DN_CHUNK
    act = _conv_silu(qkv, conv_w, lc)
    a = ab[..., :2 * H].reshape(b, t, 2, H)
    bt = ab[..., 2 * H:4 * H].reshape(b, t, 2, H)
    g = jnp.transpose(-jnp.exp(a_log) * jax.nn.softplus(a + dt_bias), (2, 0, 3, 1))
    beta = jnp.transpose(jax.nn.sigmoid(bt), (2, 0, 3, 1))
    gch = g.reshape(2, b, H, t // C, C)
    pre = jnp.cumsum(gch, axis=-1)
    gc = jnp.stack([pre[0], jnp.sum(gch[1], -1, keepdims=True) - pre[1] + gch[1]])
    bc = beta.reshape(2, b, H, t // C, C)
    rows = jnp.pad(jnp.stack([gc, bc], axis=-2), ((0, 0),) * 4 + ((0, 6), (0, 0)))
    aux = jnp.pad(jnp.stack([gc.reshape(2, b, H, t), beta], axis=-1), ((0, 0),) * 4 + ((0, 6),))
    nseg, cs = t // seg, lc // seg

    def visit_order(a_, per_seg):
        shp = a_.shape
        a_ = a_.reshape(shp[:2] + (nseg, per_seg) + shp[3:])
        a_ = jnp.concatenate([jnp.flip(a_[:, :, :cs], 2), jnp.flip(a_[:, :, cs:], 2)], axis=2)
        return a_.reshape(shp)

    rows = jnp.stack([rows[0], visit_order(rows[1], seg // C)])
    aux = jnp.stack([aux[0], visit_order(aux[1], seg)])
    return deltanet_scan(act, aux, rows, heads=H, dk=DN_DK, dv=DN_DV, chunk=C, seg=seg, ctx_segs=cs)


def _pad_cols(w, width):
    return jnp.pad(w, ((0, 0), (0, width - w.shape[1])))


def kernel(x, c, ctx, c_ctx, ada_w, ada_b, ln1_g, ln1_b, ln2_g, ln2_b, ar_w_in, mla_q_norm, mla_w_uq,
           mla_kv_norm, mla_w_ukv, ret_gn_g, ar_w_out, dn_w_in, dn_conv, dn_a_log, dn_dt_bias, dn_norm_g,
           dn_w_out, peer_w_q, peer_k1, peer_k2, peer_u, peer_v):
    bsz, L, D = x.shape
    lc = ctx.shape[1]
    T = lc + L
    tm = 256
    assert lc % tm == 0 and L % tm == 0
    cb = lc // tm
    seg = tm

    X = jnp.concatenate([ctx, x], axis=1)
    cc = jnp.zeros((8, D), F32).at[:bsz].set(c).at[bsz].set(c_ctx)
    mod_all = ada_all(cc, ada_w, ada_b)

    cos_m, sin_m = _rope_tables(L, MLA_ROPE, lc)
    cos_r, sin_r = _rope_tables(L, RET_DK, lc)

    for l in range(DEPTH):
        j = l // 2
        mod = mod_all[l]
        ml = mod[:bsz].reshape(bsz, N_MOD, D)
        mc = jnp.broadcast_to(mod[bsz].reshape(1, N_MOD, D), (bsz, N_MOD, D))
        msel = jnp.stack([mc, ml], axis=1)
        mvec = [msel[:, :, i][:, :, None, :] for i in range(N_MOD)]
        sh1, sc1, g1, sh2, sc2, g2 = mvec

        if l % 2 == 0:
            w_in = ar_w_in[j]
            w_pad = jnp.concatenate(
                [w_in[:, :416], jnp.zeros((D, 96), F32), w_in[:, 416:]], axis=1).astype(BF16)
            splits = [(0, 256), (256, 384), (384, 512), (512, 1024), (1024, 1536), (1536, 2048), (2048, 2560)]
            cq, ckv, krp, rq, rk, rv, rg = proj(X, w_pad, splits, mode="mod", shift=sh1, scale=sc1,
                                                ctx_blocks=cb, tm=tm, name="ar_in")
            (qf,) = proj(cq, mla_w_uq[j].astype(BF16), [(0, MLA_HEADS * (MLA_NOPE + MLA_ROPE))], mode="rms",
                         gain=mla_q_norm[j], tm=tm, name="mla_uq")
            (kvf,) = proj(ckv, mla_w_ukv[j].astype(BF16), [(0, MLA_HEADS * (MLA_NOPE + MLA_V))], mode="rms",
                          gain=mla_kv_norm[j], tm=tm, name="mla_ukv")
            qf = qf.reshape(bsz, T, MLA_HEADS, MLA_NOPE + MLA_ROPE)
            kvf = kvf.reshape(bsz, T, MLA_HEADS, MLA_NOPE + MLA_V)
            qn, qr = qf[..., :MLA_NOPE], qf[..., MLA_NOPE:]
            kn, vv = kvf[..., :MLA_NOPE], kvf[..., MLA_NOPE:]
            qr = _apply_rope(qr, cos_m[:, None], sin_m[:, None])
            kr = _apply_rope(krp[..., :MLA_ROPE], cos_m, sin_m)
            qh = jnp.transpose(jnp.concatenate([qn, qr], -1) * MLA_SCALE, (0, 2, 1, 3)).astype(BF16)
            kh = jnp.transpose(jnp.concatenate(
                [kn, jnp.broadcast_to(kr[:, :, None, :], (bsz, T, MLA_HEADS, MLA_ROPE))], -1), (0, 2, 1, 3)).astype(BF16)
            vh = jnp.transpose(vv, (0, 2, 1, 3)).astype(BF16)
            tk = next(c for c in (1408, 768, lc) if T % c == 0)
            o_l = attention(qh[:, :, lc:], kh, vh, tq=512, tk=tk)
            o_c = attention(qh[:, :, :lc], kh[:, :, :lc], vh[:, :, :lc], tq=lc, tk=lc)
            mla = jnp.transpose(jnp.concatenate([o_c, o_l], axis=2), (0, 2, 1, 3)).reshape(bsz, T, MLA_HEADS * MLA_V)
            ro = _retention(rq, rk, rv, cos_r, sin_r, lc, seg)
            ret = _group_norm_heads(ro, ret_gn_g[j]) * jax.nn.silu(rg)
            mix = jnp.concatenate([mla, ret], axis=-1)
            (X,) = proj(mix, ar_w_out[j].astype(BF16), [(0, D)], resid=(X, g1, ln1_g[l], ln1_b[l]), ctx_blocks=cb,
                        tm=tm, name="ar_out")
        else:
            w_pad = _pad_cols(dn_w_in[j], 4224).astype(BF16)
            splits = [(0, 3072), (3072, 4096), (4096, 4224)]
            qkv, gate, ab = proj(X, w_pad, splits, mode="mod", shift=sh1, scale=sc1, ctx_blocks=cb, tm=tm,
                                 name="dn_in")
            o_f, o_b = _deltanet(qkv, ab, dn_conv[j], dn_a_log[j], dn_dt_bias[j], lc, seg)
            (X,) = proj(o_f, dn_w_out[j].astype(BF16), [(0, D)], mode="gated_rms", other=o_b, gate=gate,
                        gain=dn_norm_g[j], group=DN_DV, resid=(X, g1, ln1_g[l], ln1_b[l]), ctx_blocks=cb, tm=tm,
                        name="dn_out")

        X = peer(X, sh2, sc2, g2, ln2_g[l], ln2_b[l], peer_w_q[l].T.astype(BF16), peer_k1[l].astype(BF16),
                 peer_k2[l].astype(BF16), peer_u[l].astype(BF16), peer_v[l].T.astype(BF16), ctx_blocks=cb)

    return X[:, lc:]
```

```python
import functools
import math

import numpy as np
import jax
import jax.numpy as jnp
from jax import lax
from jax.experimental import pallas as pl
from jax.experimental.pallas import tpu as pltpu

F32 = jnp.float32
BF16 = jnp.bfloat16

DEPTH = 4
GRID_W = 64
ROPE_BASE = 10000.0
N_MOD = 6

MLA_HEADS = 8
MLA_Q_RANK = 256
MLA_KV_RANK = 128
MLA_NOPE = 64
MLA_ROPE = 32
MLA_V = 64
MLA_SCALE = (MLA_NOPE + MLA_ROPE) ** -0.5

RET_HEADS = 8
RET_DK = 64
RET_DV = 64
RET_CHUNK = 128

DN_HEADS = 8
DN_DK = 128
DN_DV = 128
DN_CONV = 5
DN_CHUNK = 64

PEER_HEADS = 8
PEER_KEYS = 128
PEER_QDIM = 256
PEER_TOPK = 16

DEEPNORM_ALPHA = (2 * DEPTH) ** 0.25

LANE = 128
VMEM_LIMIT = 56 * 1024 * 1024

NT = (((1,), (1,)), ((), ()))
TN = (((0,), (0,)), ((), ()))


def _cparams(sem):
    return pltpu.CompilerParams(dimension_semantics=sem, vmem_limit_bytes=VMEM_LIMIT)


def _ada_kernel(c_ref, w_ref, b_ref, o_ref):
    c = c_ref[...]
    a = (c * jax.nn.sigmoid(c)).astype(BF16)
    o_ref[...] = jnp.dot(a, w_ref[...].astype(BF16), preferred_element_type=F32) + b_ref[...]


def ada_all(cc, ada_w, ada_b, tn=1024):
    depth, d, n = ada_w.shape
    m = cc.shape[0]
    return pl.pallas_call(
        _ada_kernel,
        grid=(depth, n // tn),
        in_specs=[
            pl.BlockSpec((m, d), lambda l, j: (0, 0)),
            pl.BlockSpec((None, d, tn), lambda l, j: (l, 0, j)),
            pl.BlockSpec((None, 1, tn), lambda l, j: (l, 0, j)),
        ],
        out_specs=pl.BlockSpec((None, m, tn), lambda l, j: (l, 0, j)),
        out_shape=jax.ShapeDtypeStruct((depth, m, n), F32),
        compiler_params=_cparams(("arbitrary", "arbitrary")),
        name="ada",
    )(cc, ada_w, ada_b.reshape(depth, 1, n))


def _resid_ln(x, y, gate, g, b, eps):
    v = DEEPNORM_ALPHA * x + gate * y
    vc = v - jnp.mean(v, axis=-1, keepdims=True)
    var = jnp.mean(vc * vc, axis=-1, keepdims=True)
    return vc * lax.rsqrt(var + eps) * g + b


def _proj_kernel(*refs, mode, splits, eps, group, resid, ln_eps):
    n_in = {"mod": 4, "rms": 3, "gated_rms": 5, "none": 2}[mode]
    ins, rest = refs[:n_in], refs[n_in:]
    w_ref = ins[-1]
    if mode == "mod":
        x_ref, sh_ref, sc_ref = ins[:3]
        x = x_ref[...] * (1.0 + sc_ref[...]) + sh_ref[...]
    elif mode == "rms":
        x_ref, g_ref = ins[:2]
        x = x_ref[...]
        x = x * lax.rsqrt(jnp.mean(x * x, axis=-1, keepdims=True) + eps) * g_ref[...]
    elif mode == "gated_rms":
        a_ref, b_ref, gate_ref, g_ref = ins[:4]
        o = a_ref[...] + b_ref[...]
        gate = gate_ref[...]
        parts = []
        for h in range(o.shape[1] // group):
            oh = o[:, h * group:(h + 1) * group]
            parts.append(oh * lax.rsqrt(jnp.mean(oh * oh, axis=-1, keepdims=True) + eps))
        x = jnp.concatenate(parts, axis=1) * g_ref[...] * (gate * jax.nn.sigmoid(gate))
    else:
        x = ins[0][...]
    z = jnp.dot(x.astype(BF16), w_ref[...], preferred_element_type=F32)
    if resid:
        xs_ref, gt_ref, lg_ref, lb_ref, o_ref = rest
        o_ref[...] = _resid_ln(xs_ref[...], z, gt_ref[...], lg_ref[...], lb_ref[...], ln_eps)
        return
    for o_ref, (a, b) in zip(rest, splits):
        o_ref[...] = z[:, a:b].astype(o_ref.dtype)


def proj(x, w, splits, *, mode="none", shift=None, scale=None, gain=None, other=None, gate=None, group=LANE,
         resid=None, ctx_blocks=1, tm=256, eps=1e-6, ln_eps=1e-5, name="proj"):
    bsz, t, k = x.shape
    n = w.shape[1]
    row = lambda b, j: (b, j, 0)
    sel = lambda b, j: (b, jnp.minimum(j // ctx_blocks, 1), 0, 0)
    vec = lambda b, j: (0, 0)
    in_specs = [pl.BlockSpec((None, tm, k), row)]
    args = [x]
    if mode == "mod":
        in_specs += [pl.BlockSpec((None, None, 1, k), sel)] * 2
        args += [shift, scale]
    elif mode == "rms":
        in_specs += [pl.BlockSpec((1, k), vec)]
        args += [gain.reshape(1, k)]
    elif mode == "gated_rms":
        in_specs += [pl.BlockSpec((None, tm, k), row)] * 2 + [pl.BlockSpec((1, k), vec)]
        args += [other, gate, jnp.tile(gain, k // group).reshape(1, k)]
    in_specs += [pl.BlockSpec((k, n), vec)]
    args += [w]
    if resid is not None:
        stream, gate_sel, ln_g, ln_b = resid
        assert list(splits) == [(0, n)] and stream.shape[-1] == n
        in_specs += [pl.BlockSpec((None, tm, n), row), pl.BlockSpec((None, None, 1, n), sel),
                     pl.BlockSpec((1, n), vec), pl.BlockSpec((1, n), vec)]
        args += [stream, gate_sel, ln_g.reshape(1, n), ln_b.reshape(1, n)]
    return pl.pallas_call(
        functools.partial(_proj_kernel, mode=mode, splits=tuple(splits), eps=eps, group=group,
                          resid=resid is not None, ln_eps=ln_eps),
        grid=(bsz, t // tm),
        in_specs=in_specs,
        out_specs=[pl.BlockSpec((None, tm, b - a), row) for a, b in splits],
        out_shape=[jax.ShapeDtypeStruct((bsz, t, b - a), F32) for a, b in splits],
        compiler_params=_cparams(("parallel", "parallel")),
        name=name,
    )(*args)


def _attn_kernel(q_ref, k_ref, v_ref, o_ref, *, tk, nk, hb):
    tq = q_ref.shape[1]
    dv = v_ref.shape[-1]
    qs = [q_ref[h] for h in range(hb)]

    def body(i, carry):
        start = pl.multiple_of(i * tk, tk)
        ss = [lax.dot_general(qs[h], k_ref[h, pl.ds(start, tk), :], NT, preferred_element_type=F32)
              for h in range(hb)]
        ps, new = [], []
        for h in range(hb):
            m, l, acc = carry[h]
            m_new = jnp.maximum(m, jnp.max(ss[h], axis=1, keepdims=True))
            p = jnp.exp(ss[h] - m_new)
            alpha = jnp.exp(m - m_new)
            ps.append(p.astype(BF16))
            new.append((m_new, alpha * l + jnp.sum(p, axis=1, keepdims=True), alpha * acc))
        out = []
        for h in range(hb):
            m_new, l, acc = new[h]
            out.append((m_new, l, acc + jnp.dot(ps[h], v_ref[h, pl.ds(start, tk), :], preferred_element_type=F32)))
        return tuple(out)

    init = tuple((jnp.full((tq, 1), -jnp.inf, F32), jnp.zeros((tq, 1), F32), jnp.zeros((tq, dv), F32))
                 for _ in range(hb))
    fin = lax.fori_loop(0, nk, body, init)
    for h in range(hb):
        _, l, acc = fin[h]
        o_ref[h] = acc / l


def attention(q, k, v, *, tq, tk, hb=2):
    bsz, h, lq, dq = q.shape
    lk, dv = v.shape[2], v.shape[3]
    assert lq % tq == 0 and lk % tk == 0 and h % hb == 0
    return pl.pallas_call(
        functools.partial(_attn_kernel, tk=tk, nk=lk // tk, hb=hb),
        grid=(bsz, h // hb, lq // tq),
        in_specs=[pl.BlockSpec((None, hb, tq, dq), lambda b, hh, i: (b, hh, i, 0)),
                  pl.BlockSpec((None, hb, lk, dq), lambda b, hh, i: (b, hh, 0, 0)),
                  pl.BlockSpec((None, hb, lk, dv), lambda b, hh, i: (b, hh, 0, 0))],
        out_specs=pl.BlockSpec((None, hb, tq, dv), lambda b, hh, i: (b, hh, i, 0)),
        out_shape=jax.ShapeDtypeStruct((bsz, h, lq, dv), F32),
        compiler_params=_cparams(("parallel", "parallel", "parallel")),
        name="mla_attn",
    )(q, k, v)


def _top_values(s, k):
    vals = []
    cur = s
    for i in range(k):
        m = jnp.max(cur, axis=0, keepdims=True)
        vals.append(m)
        if i + 1 < k:
            cur = jnp.where(cur >= m, -jnp.inf, cur)
    return vals


def _gelu_exact(x):
    return 0.5 * x * (1.0 + lax.erf(x * (2.0 ** -0.5)))


def _peer_prepare(x_ref, sh_ref, sc_ref, wqt_ref, k1_ref, k2_ref, xb_ref, thr_ref, s2_ref, e1_ref, e2_ref):
    half = PEER_QDIM // 2
    K = PEER_TOPK
    H = PEER_HEADS
    x = x_ref[...] * (1.0 + sc_ref[...]) + sh_ref[...]
    xb = x.astype(BF16)
    xb_ref[...] = xb
    qt = lax.dot_general(wqt_ref[...], xb, NT, preferred_element_type=F32)
    v1s, v2s = [], []
    for h in range(H):
        q1 = qt[h * PEER_QDIM: h * PEER_QDIM + half].astype(BF16)
        q2 = qt[h * PEER_QDIM + half: (h + 1) * PEER_QDIM].astype(BF16)
        s1 = jnp.dot(k1_ref[h], q1, preferred_element_type=F32)
        s2 = jnp.dot(k2_ref[h], q2, preferred_element_type=F32)
        thr_ref[h] = s1
        s2_ref[h] = s2
        v1s.append(_top_values(s1, K + 1))
        v2s.append(_top_values(s2, K + 1))
    v1 = [jnp.concatenate([v1s[h][a] for h in range(H)], axis=0) for a in range(K + 1)]
    v2 = [jnp.concatenate([v2s[h][b] for h in range(H)], axis=0) for b in range(K + 1)]
    cands = [v1[a] + v2[b] for a in range(K + 1) for b in range(K + 1) if (a + 1) * (b + 1) <= K + 1]
    cur = list(cands)
    for i in range(K + 1):
        m = functools.reduce(jnp.maximum, cur)
        if i == K - 1:
            kth = m
        if i < K:
            cur = [jnp.where(c >= m, -jnp.inf, c) for c in cur]
    tau = 0.5 * (kth + m)
    top = cands[0]
    z = functools.reduce(jnp.add, [jnp.where(c >= tau, jnp.exp(c - top), 0.0) for c in cands])
    for h in range(H):
        s1 = thr_ref[h]
        thr_ref[h] = tau[h:h + 1] - s1
        e1_ref[h] = jnp.exp(s1 - v1s[h][0]) / z[h:h + 1]
        e2_ref[h] = jnp.exp(s2_ref[h] - v2s[h][0])


def _peer_weights(blk, rows, row_lo, row_hi, thr_ref, s2_ref, e1_ref, e2_ref, ht_ref, a_ref):
    tb = ht_ref.shape[1]
    r0 = blk * rows
    for r in range(row_lo, row_hi):
        rs = slice(r * PEER_KEYS, (r + 1) * PEER_KEYS)
        thr_rows = [thr_ref[h, pl.ds(r0 + r, 1), :] for h in range(PEER_HEADS)]
        e1_rows = [e1_ref[h, pl.ds(r0 + r, 1), :] for h in range(PEER_HEADS)]
        for lt in range(tb // LANE):
            ls = slice(lt * LANE, (lt + 1) * LANE)
            w = None
            for h in range(PEER_HEADS):
                term = jnp.where(s2_ref[h, :, ls] >= thr_rows[h][:, ls], e1_rows[h][:, ls] * e2_ref[h, :, ls], 0.0)
                w = term if w is None else w + term
            a_ref[rs, ls] = (w * _gelu_exact(ht_ref[rs, ls])).astype(BF16)


def _peer_kernel(x_ref, sh_ref, sc_ref, gt_ref, lg_ref, lb_ref, wqt_ref, k1_ref, k2_ref,
                 u0_ref, ua_ref, ub_ref, vta_ref, vtb_ref, o_ref,
                 xb_ref, thr_ref, s2_ref, e1_ref, e2_ref, hte_ref, hto_ref, ae_ref, ao_ref, acc_ref, *, rows, ln_eps):
    g = pl.program_id(2)
    last = pl.num_programs(2) - 1
    p = g % 2
    q = 1 - p
    tabs = (thr_ref, s2_ref, e1_ref, e2_ref)

    @pl.when(g == 0)
    def _first():
        _peer_prepare(x_ref, sh_ref, sc_ref, wqt_ref, k1_ref, k2_ref, xb_ref, *tabs)
        acc_ref[...] = jnp.zeros_like(acc_ref)
        ao_ref[1] = jnp.zeros(ao_ref.shape[1:], ao_ref.dtype)
        hte_ref[0] = lax.dot_general(u0_ref[...], xb_ref[...], NT, preferred_element_type=F32)

    @pl.when(g < last)
    def _even():
        hto_ref[...] = lax.dot_general(ua_ref[...], xb_ref[...], NT, preferred_element_type=F32)
        acc_ref[...] += jnp.dot(vta_ref[...], ao_ref[q], preferred_element_type=F32)
        _peer_weights(2 * g, rows, 0, rows, *tabs, hte_ref.at[p], ae_ref)

    @pl.when(g < last)
    def _odd():
        hte_ref[q] = lax.dot_general(ub_ref[...], xb_ref[...], NT, preferred_element_type=F32)
        acc_ref[...] += jnp.dot(vtb_ref[...], ae_ref[...], preferred_element_type=F32)
        _peer_weights(2 * g + 1, rows, 0, rows, *tabs, hto_ref, ao_ref.at[p])

    @pl.when(g == last)
    def _finish():
        acc = acc_ref[...] + jnp.dot(vta_ref[...], ao_ref[q], preferred_element_type=F32)
        o_ref[...] = _resid_ln(x_ref[...], acc.T, gt_ref[...], lg_ref[...], lb_ref[...], ln_eps)


def peer(x, shift, scale, gate, ln_g, ln_b, wqt, k1, k2, u, vt, *, ctx_blocks, tb=256, eb=1024, ln_eps=1e-5):
    bsz, t, d = x.shape
    n = u.shape[0]
    ne = n // eb
    assert ne % 2 == 0
    rows = eb // PEER_KEYS
    row = lambda b, j, g: (b, j, 0)
    sel = lambda b, j, g: (b, jnp.minimum(j // ctx_blocks, 1), 0, 0)
    const2 = lambda b, j, g: (0, 0)
    const3 = lambda b, j, g: (0, 0, 0)
    tab = pltpu.VMEM((PEER_HEADS, PEER_KEYS, tb), F32)
    ublk = lambda f: pl.BlockSpec((eb, d), lambda b, j, g: (f(g), 0))
    vblk = lambda f: pl.BlockSpec((d, eb), lambda b, j, g: (0, f(g)))
    return pl.pallas_call(
        functools.partial(_peer_kernel, rows=rows, ln_eps=ln_eps),
        grid=(bsz, t // tb, ne // 2 + 1),
        in_specs=[pl.BlockSpec((None, tb, d), row),
                  pl.BlockSpec((None, None, 1, d), sel), pl.BlockSpec((None, None, 1, d), sel),
                  pl.BlockSpec((None, None, 1, d), sel), pl.BlockSpec((1, d), const2), pl.BlockSpec((1, d), const2),
                  pl.BlockSpec(wqt.shape, const2),
                  pl.BlockSpec(k1.shape, const3), pl.BlockSpec(k2.shape, const3),
                  ublk(lambda g: 0),
                  ublk(lambda g: jnp.minimum(2 * g + 1, ne - 1)),
                  ublk(lambda g: jnp.minimum(2 * g + 2, ne - 1)),
                  vblk(lambda g: jnp.maximum(2 * g - 1, 0)),
                  vblk(lambda g: jnp.minimum(2 * g, ne - 1))],
        out_specs=pl.BlockSpec((None, tb, d), row),
        out_shape=jax.ShapeDtypeStruct((bsz, t, d), F32),
        scratch_shapes=[pltpu.VMEM((tb, d), BF16), tab, tab, tab, tab,
                        pltpu.VMEM((2, eb, tb), F32), pltpu.VMEM((eb, tb), F32),
                        pltpu.VMEM((eb, tb), BF16), pltpu.VMEM((2, eb, tb), BF16),
                        pltpu.VMEM((d, tb), F32)],
        compiler_params=_cparams(("parallel", "parallel", "arbitrary")),
        name="peer",
    )(x, shift, scale, gate, ln_g.reshape(1, d), ln_b.reshape(1, d), wqt, k1, k2, u, u, u, vt, vt)


def _bwd_seg(s, cs, nseg):
    return jnp.where(s < cs, cs - 1 - s, nseg - 1 - (s - cs))


def _ret_kernel(qf_ref, kf_ref, vf_ref, qb_ref, kb_ref, vb_ref, d_ref, qdec_ref, kdec_ref, cdec_ref,
                of_ref, ob_ref, s_ref, *, chunk, nchunk, hb):
    @pl.when(pl.program_id(2) == 0)
    def _init():
        s_ref[...] = jnp.zeros_like(s_ref)

    ins = ((qf_ref, kf_ref, vf_ref, of_ref), (qb_ref, kb_ref, vb_ref, ob_ref))
    chains = [(d, j) for d in range(2) for j in range(hb)]

    def step(i, carry):
        sls = [pl.ds(pl.multiple_of((i if d == 0 else nchunk - 1 - i) * chunk, chunk), chunk) for d, _ in chains]
        qs = [ins[d][0][j, sl, :] for (d, j), sl in zip(chains, sls)]
        ks = [ins[d][1][j, sl, :] for (d, j), sl in zip(chains, sls)]
        vs = [ins[d][2][j, sl, :] for (d, j), sl in zip(chains, sls)]
        scores = [lax.dot_general(q, k, NT, preferred_element_type=F32) * d_ref[d, j]
                  for (d, j), q, k in zip(chains, qs, ks)]
        inter = [jnp.dot((q.astype(F32) * qdec_ref[d, j]).astype(BF16), s.astype(BF16), preferred_element_type=F32)
                 for (d, j), q, s in zip(chains, qs, carry)]
        upd = [lax.dot_general((k.astype(F32) * kdec_ref[d, j]).astype(BF16), v, TN, preferred_element_type=F32)
               for (d, j), k, v in zip(chains, ks, vs)]
        for (d, j), sl, sc, v, it in zip(chains, sls, scores, vs, inter):
            ins[d][3][j, sl, :] = jnp.dot(sc.astype(BF16), v, preferred_element_type=F32) + it
        return tuple(s * cdec_ref[j] + u for (d, j), s, u in zip(chains, carry, upd))

    final = lax.fori_loop(0, nchunk, step, tuple(s_ref[d, j] for d, j in chains))
    for (d, j), s in zip(chains, final):
        s_ref[d, j] = s


def retention_scan(q, k, v, dmat, qdec, kdec, cdec, *, chunk, seg, ctx_segs, hb=4):
    bsz, h, t, dk = q.shape
    dv = v.shape[-1]
    assert t % seg == 0 and seg % chunk == 0 and h % hb == 0
    nseg = t // seg
    fwd = lambda d_: pl.BlockSpec((None, hb, seg, d_), lambda b, hh, s: (b, hh, s, 0))
    bwd = lambda d_: pl.BlockSpec((None, hb, seg, d_), lambda b, hh, s: (b, hh, _bwd_seg(s, ctx_segs, nseg), 0))
    per_head = lambda *shape: pl.BlockSpec((2, hb) + shape, lambda b, hh, s: (0, hh, 0, 0))
    return pl.pallas_call(
        functools.partial(_ret_kernel, chunk=chunk, nchunk=seg // chunk, hb=hb),
        grid=(bsz, h // hb, nseg),
        in_specs=[fwd(dk), fwd(dk), fwd(dv), bwd(dk), bwd(dk), bwd(dv),
                  per_head(chunk, chunk), per_head(chunk, dk), per_head(chunk, dk),
                  pl.BlockSpec((hb, dk, dv), lambda b, hh, s: (hh, 0, 0))],
        out_specs=[fwd(dv), bwd(dv)],
        out_shape=[jax.ShapeDtypeStruct((bsz, h, t, dv), F32)] * 2,
        scratch_shapes=[pltpu.VMEM((2, hb, dk, dv), F32)],
        compiler_params=_cparams(("parallel", "parallel", "arbitrary")),
        name="retention",
    )(q, k, v, q, k, v, dmat, qdec, kdec, cdec)


def _dn_kernel(qf_ref, kf_ref, vf_ref, qb_ref, kb_ref, vb_ref, aux_ref, rows_ref, of_ref, ob_ref,
               s_ref, m_s, b_s, qp_s, cd_s, sall_s, *, chunk, nchunk, hb, eps):
    C = chunk

    @pl.when(pl.program_id(2) == 0)
    def _init():
        s_ref[...] = jnp.zeros_like(s_ref)

    ri = lax.broadcasted_iota(jnp.int32, (C, C), 0)
    ci = lax.broadcasted_iota(jnp.int32, (C, C), 1)
    eye = (ri == ci).astype(F32)
    bmm = lambda a, b: jnp.einsum('nij,njk->nik', a.astype(BF16), b.astype(BF16), preferred_element_type=F32)
    bmm_nt = lambda a, b: jnp.einsum('nid,njd->nij', a.astype(BF16), b.astype(BF16),
                                     preferred_element_type=F32)
    bmm_tn = lambda a, b: jnp.einsum('nci,ncj->nij', a.astype(BF16), b.astype(BF16),
                                     preferred_element_type=F32)
    ins = ((qf_ref, kf_ref, vf_ref, of_ref), (qb_ref, kb_ref, vb_ref, ob_ref))
    dk = qf_ref.shape[-1] // hb
    dv = vf_ref.shape[-1] // hb
    chains = [(d, j) for d in range(2) for j in range(hb)]
    nb = len(chains) * nchunk

    def gather(fn):
        return jnp.concatenate([fn(d, j) for d, j in chains], axis=0)

    def l2n(ref, j, scale):
        t = ref[:, j * dk:(j + 1) * dk]
        return (t * (lax.rsqrt(jnp.sum(t * t, axis=-1, keepdims=True) + eps) * scale)).reshape(nchunk, C, dk)

    q = gather(lambda d, j: l2n(ins[d][0], j, dk ** -0.5))
    k = gather(lambda d, j: l2n(ins[d][1], j, 1.0))
    v = gather(lambda d, j: ins[d][2][:, j * dv:(j + 1) * dv].reshape(nchunk, C, dv))
    aux = gather(lambda d, j: aux_ref[d, j].reshape(nchunk, C, aux_ref.shape[-1]))
    rws = gather(lambda d, j: rows_ref[d, j])
    gcol = aux[:, :, 0:1]
    bcol = aux[:, :, 1:2]
    grow = rws[:, 0:1, :]
    brow = rws[:, 1:2, :]
    is_fwd = lax.broadcasted_iota(jnp.int32, (nb, 1, 1), 0) < hb * nchunk
    signed = (ri - ci) * jnp.where(is_fwd, 1, -1)
    incl = signed >= 0
    strict = signed > 0
    kk = bmm_nt(k, k)
    qk = bmm_nt(q, k)
    decay = jnp.where(incl, jnp.exp(jnp.where(incl, gcol - grow, 0.0)), 0.0)
    x = jnp.where(strict, -(kk * decay * bcol), 0.0)
    x2 = bmm(x, x)
    x4 = bmm(x2, x2)
    x8 = bmm(x4, x4)
    x16 = bmm(x8, x8)
    x32 = bmm(x16, x16)
    p1 = eye + x + x2 + bmm(x, x2)
    p2 = eye + x4 + x8 + bmm(x4, x8)
    p3 = eye + x16 + x32 + bmm(x16, x32)
    tinv = bmm(bmm(p1, p2), p3)
    egrow = jnp.exp(grow)
    u = bmm(tinv * brow, v)
    w = bmm(tinv * (brow * egrow), k)
    attn = qk * decay
    gtot = jnp.where(is_fwd, grow[:, :, C - 1:C], grow[:, :, 0:1])
    qd = q * jnp.exp(gcol)
    kd = k * jnp.exp(gtot - gcol)
    per_chain = lambda t: t.reshape((2, hb, nchunk) + t.shape[1:])
    m_s[...] = per_chain(bmm_tn(kd, w).astype(BF16))
    b_s[...] = per_chain(bmm_tn(kd, u))
    qp_s[...] = per_chain((qd - bmm(attn, w)).astype(BF16))
    cd_s[...] = per_chain(jnp.broadcast_to(jnp.exp(gtot), (nb, 1, dv)))
    o_local = bmm(attn, u)

    def step(i, carry):
        new = []
        for (d, j), s in zip(chains, carry):
            c = i if d == 0 else nchunk - 1 - i
            sb = s.astype(BF16)
            sall_s[d, j, c] = sb
            new.append(s * cd_s[d, j, c] - jnp.dot(m_s[d, j, c], sb, preferred_element_type=F32) + b_s[d, j, c])
        return tuple(new)

    final = lax.fori_loop(0, nchunk, step, tuple(s_ref[d, j] for d, j in chains))
    for (d, j), s in zip(chains, final):
        s_ref[d, j] = s

    o_all = o_local + bmm(qp_s[...].reshape(nb, C, dk), sall_s[...].reshape(nb, dk, dv))
    for idx, (d, j) in enumerate(chains):
        ins[d][3][:, j * dv:(j + 1) * dv] = o_all[idx * nchunk:(idx + 1) * nchunk].reshape(nchunk * C, dv)


def deltanet_scan(qkv, aux, rows, *, heads, dk, dv, chunk, seg, ctx_segs, hb=4, eps=1e-6):
    bsz, t, _ = qkv.shape
    assert dk == dv == LANE and t % seg == 0 and seg % chunk == 0 and heads % hb == 0
    nseg = t // seg
    nchunk = seg // chunk
    hg = heads // hb
    fwd = lambda off: pl.BlockSpec((None, seg, hb * dk), lambda b, hh, s: (b, s, off + hh))
    bwd = lambda off: pl.BlockSpec((None, seg, hb * dk), lambda b, hh, s: (b, _bwd_seg(s, ctx_segs, nseg), off + hh))
    return pl.pallas_call(
        functools.partial(_dn_kernel, chunk=chunk, nchunk=nchunk, hb=hb, eps=eps),
        grid=(bsz, hg, nseg),
        in_specs=[fwd(0), fwd(hg), fwd(2 * hg), bwd(0), bwd(hg), bwd(2 * hg),
                  pl.BlockSpec((2, None, hb, seg, aux.shape[-1]), lambda b, hh, s: (0, b, hh, s, 0)),
                  pl.BlockSpec((2, None, hb, nchunk, 8, chunk), lambda b, hh, s: (0, b, hh, s, 0, 0))],
        out_specs=[fwd(0), bwd(0)],
        out_shape=[jax.ShapeDtypeStruct((bsz, t, heads * dv), F32)] * 2,
        scratch_shapes=[pltpu.VMEM((2, hb, dk, dv), F32),
                        pltpu.VMEM((2, hb, nchunk, dk, dv), BF16),
                        pltpu.VMEM((2, hb, nchunk, dk, dv), F32),
                        pltpu.VMEM((2, hb, nchunk, chunk, dk), BF16),
                        pltpu.VMEM((2, hb, nchunk, 1, dv), F32),
                        pltpu.VMEM((2, hb, nchunk, dk, dv), BF16)],
        compiler_params=_cparams(("parallel", "parallel", "arbitrary")),
        name="deltanet",
    )(qkv, qkv, qkv, qkv, qkv, qkv, aux, rows)


def _rope_tables(length, dim, ctx_len):
    rows = length // GRID_W
    n_freq = dim // 4
    inv = ROPE_BASE ** (-jnp.arange(n_freq, dtype=F32) / n_freq)
    r = jnp.repeat(jnp.arange(rows, dtype=F32), GRID_W)
    c = jnp.tile(jnp.arange(GRID_W, dtype=F32), rows)
    ang = jnp.concatenate([r[:, None] * inv, c[:, None] * inv], axis=-1)
    cos = jnp.concatenate([jnp.ones((ctx_len, dim // 2), F32), jnp.cos(ang)], axis=0)
    sin = jnp.concatenate([jnp.zeros((ctx_len, dim // 2), F32), jnp.sin(ang)], axis=0)
    return cos, sin


def _apply_rope(x, cos, sin):
    x1, x2 = jnp.split(x, 2, axis=-1)
    return jnp.concatenate([x1 * cos - x2 * sin, x2 * cos + x1 * sin], axis=-1)


def _retention(rq, rk, rv, cos, sin, lc, seg):
    b, t, _ = rq.shape
    q = _apply_rope(rq.reshape(b, t, RET_HEADS, RET_DK), cos[:, None], sin[:, None])
    k = _apply_rope(rk.reshape(b, t, RET_HEADS, RET_DK), cos[:, None], sin[:, None]) * RET_DK ** -0.5
    v = rv.reshape(b, t, RET_HEADS, RET_DV)
    tr = lambda a: jnp.transpose(a, (0, 2, 1, 3)).astype(BF16)
    lg = jnp.log1p(-jnp.exp2(-5.0 - jnp.arange(RET_HEADS, dtype=F32)))[:, None, None]
    C = RET_CHUNK
    pos = jnp.arange(C, dtype=F32)
    dist = pos[:, None] - pos[None, :]
    dm = lambda dd, mask: jnp.where(mask, jnp.exp(jnp.where(mask, dd, 0.0) * lg), 0.0)
    dmat = jnp.stack([dm(dist, dist >= 0), dm(-dist, dist < 0)])
    col = lambda e: jnp.broadcast_to(jnp.exp(e * lg[:, :, 0])[..., None], (RET_HEADS, C, RET_DK))
    qdec = jnp.stack([col(pos + 1.0), col(C - pos)])
    kdec = jnp.stack([col(C - 1.0 - pos), col(pos)])
    cdec = jnp.broadcast_to(jnp.exp(C * lg), (RET_HEADS, RET_DK, RET_DV))
    o_f, o_b = retention_scan(tr(q), tr(k), tr(v), dmat, qdec, kdec, cdec, chunk=C, seg=seg, ctx_segs=lc // seg)
    return o_f + o_b


def _group_norm_heads(o, g, eps=1e-5):
    oc = o - jnp.mean(o, -1, keepdims=True)
    y = oc * lax.rsqrt(jnp.mean(oc * oc, -1, keepdims=True) + eps)
    b, h, L, dv = o.shape
    return jnp.transpose(y, (0, 2, 1, 3)).reshape(b, L, h * dv) * g


def _conv_silu(x, w, lc):
    b, t, ch = x.shape
    K = w.shape[0]
    half = (K - 1) // 2
    xp = jnp.pad(x, ((0, 0), (half, K // 2), (0, 0)))
    pos = jnp.arange(t)
    acc = 0.0
    for j in range(K):
        src = pos + (j - half)
        ok = (src >= 0) & (src < t) & ((pos < lc) == (src < lc))
        acc = acc + jnp.where(ok[None, :, None], xp[:, j:j + t], 0.0) * w[j]
    return acc * jax.nn.sigmoid(acc)


def _deltanet(qkv, ab, conv_w, a_log, dt_bias, lc, seg):
    b, t, _ = qkv.shape
    H, C = DN_HEADS, DN_CHUNK
    act = _conv_silu(qkv, conv_w, lc)
    a = ab[..., :2 * H].reshape(b, t, 2, H)
    bt = ab[..., 2 * H:4 * H].reshape(b, t, 2, H)
    g = jnp.transpose(-jnp.exp(a_log) * jax.nn.softplus(a + dt_bias), (2, 0, 3, 1))
    beta = jnp.transpose(jax.nn.sigmoid(bt), (2, 0, 3, 1))
    gch = g.reshape(2, b, H, t // C, C)
    pre = jnp.cumsum(gch, axis=-1)
    gc = jnp.stack([pre[0], jnp.sum(gch[1], -1, keepdims=True) - pre[1] + gch[1]])
    bc = beta.reshape(2, b, H, t // C, C)
    rows = jnp.pad(jnp.stack([gc, bc], axis=-2), ((0, 0),) * 4 + ((0, 6), (0, 0)))
    aux = jnp.pad(jnp.stack([gc.reshape(2, b, H, t), beta], axis=-1), ((0, 0),) * 4 + ((0, 6),))
    nseg, cs = t // seg, lc // seg

    def visit_order(a_, per_seg):
        shp = a_.shape
        a_ = a_.reshape(shp[:2] + (nseg, per_seg) + shp[3:])
        a_ = jnp.concatenate([jnp.flip(a_[:, :, :cs], 2), jnp.flip(a_[:, :, cs:], 2)], axis=2)
        return a_.reshape(shp)

    rows = jnp.stack([rows[0], visit_order(rows[1], seg // C)])
    aux = jnp.stack([aux[0], visit_order(aux[1], seg)])
    return deltanet_scan(act, aux, rows, heads=H, dk=DN_DK, dv=DN_DV, chunk=C, seg=seg, ctx_segs=cs)


def _pad_cols(w, width):
    return jnp.pad(w, ((0, 0), (0, width - w.shape[1])))


def kernel(x, c, ctx, c_ctx, ada_w, ada_b, ln1_g, ln1_b, ln2_g, ln2_b, ar_w_in, mla_q_norm, mla_w_uq,
           mla_kv_norm, mla_w_ukv, ret_gn_g, ar_w_out, dn_w_in, dn_conv, dn_a_log, dn_dt_bias, dn_norm_g,
           dn_w_out, peer_w_q, peer_k1, peer_k2, peer_u, peer_v):
    bsz, L, D = x.shape
    lc = ctx.shape[1]
    T = lc + L
    tm = 256
    assert lc % tm == 0 and L % tm == 0
    cb = lc // tm
    seg = tm

    X = jnp.concatenate([ctx, x], axis=1)
    cc = jnp.zeros((8, D), F32).at[:bsz].set(c).at[bsz].set(c_ctx)
    mod_all = ada_all(cc, ada_w, ada_b)

    cos_m, sin_m = _rope_tables(L, MLA_ROPE, lc)
    cos_r, sin_r = _rope_tables(L, RET_DK, lc)

    for l in range(DEPTH):
        j = l // 2
        mod = mod_all[l]
        ml = mod[:bsz].reshape(bsz, N_MOD, D)
        mc = jnp.broadcast_to(mod[bsz].reshape(1, N_MOD, D), (bsz, N_MOD, D))
        msel = jnp.stack([mc, ml], axis=1)
        mvec = [msel[:, :, i][:, :, None, :] for i in range(N_MOD)]
        sh1, sc1, g1, sh2, sc2, g2 = mvec

        if l % 2 == 0:
            w_in = ar_w_in[j]
            w_pad = jnp.concatenate(
                [w_in[:, :416], jnp.zeros((D, 96), F32), w_in[:, 416:]], axis=1).astype(BF16)
            splits = [(0, 256), (256, 384), (384, 512), (512, 1024), (1024, 1536), (1536, 2048), (2048, 2560)]
            cq, ckv, krp, rq, rk, rv, rg = proj(X, w_pad, splits, mode="mod", shift=sh1, scale=sc1,
                                                ctx_blocks=cb, tm=tm, name="ar_in")
            (qf,) = proj(cq, mla_w_uq[j].astype(BF16), [(0, MLA_HEADS * (MLA_NOPE + MLA_ROPE))], mode="rms",
                         gain=mla_q_norm[j], tm=tm, name="mla_uq")
            (kvf,) = proj(ckv, mla_w_ukv[j].astype(BF16), [(0, MLA_HEADS * (MLA_NOPE + MLA_V))], mode="rms",
                          gain=mla_kv_norm[j], tm=tm, name="mla_ukv")
            qf = qf.reshape(bsz, T, MLA_HEADS, MLA_NOPE + MLA_ROPE)
            kvf = kvf.reshape(bsz, T, MLA_HEADS, MLA_NOPE + MLA_V)
            qn, qr = qf[..., :MLA_NOPE], qf[..., MLA_NOPE:]
            kn, vv = kvf[..., :MLA_NOPE], kvf[..., MLA_NOPE:]
            qr = _apply_rope(qr, cos_m[:, None], sin_m[:, None])
            kr = _apply_rope(krp[..., :MLA_ROPE], cos_m, sin_m)
            qh = jnp.transpose(jnp.concatenate([qn, qr], -1) * MLA_SCALE, (0, 2, 1, 3)).astype(BF16)
            kh = jnp.transpose(jnp.concatenate(
                [kn, jnp.broadcast_to(kr[:, :, None, :], (bsz, T, MLA_HEADS, MLA_ROPE))], -1), (0, 2, 1, 3)).astype(BF16)
            vh = jnp.transpose(vv, (0, 2, 1, 3)).astype(BF16)
            tk = next(c for c in (1408, 768, lc) if T % c == 0)
            o_l = attention(qh[:, :, lc:], kh, vh, tq=1024 if L % 1024 == 0 else 512, tk=tk)
            o_c = attention(qh[:, :, :lc], kh[:, :, :lc], vh[:, :, :lc], tq=lc, tk=lc)
            mla = jnp.transpose(jnp.concatenate([o_c, o_l], axis=2), (0, 2, 1, 3)).reshape(bsz, T, MLA_HEADS * MLA_V)
            ro = _retention(rq, rk, rv, cos_r, sin_r, lc, seg)
            ret = _group_norm_heads(ro, ret_gn_g[j]) * jax.nn.silu(rg)
            mix = jnp.concatenate([mla, ret], axis=-1)
            (X,) = proj(mix, ar_w_out[j].astype(BF16), [(0, D)], resid=(X, g1, ln1_g[l], ln1_b[l]), ctx_blocks=cb,
                        tm=tm, name="ar_out")
        else:
            w_pad = _pad_cols(dn_w_in[j], 4224).astype(BF16)
            splits = [(0, 3072), (3072, 4096), (4096, 4224)]
            qkv, gate, ab = proj(X, w_pad, splits, mode="mod", shift=sh1, scale=sc1, ctx_blocks=cb, tm=tm,
                                 name="dn_in")
            o_f, o_b = _deltanet(qkv, ab, dn_conv[j], dn_a_log[j], dn_dt_bias[j], lc, seg)
            (X,) = proj(o_f, dn_w_out[j].astype(BF16), [(0, D)], mode="gated_rms", other=o_b, gate=gate,
                        gain=dn_norm_g[j], group=DN_DV, resid=(X, g1, ln1_g[l], ln1_b[l]), ctx_blocks=cb, tm=tm,
                        name="dn_out")

        X = peer(X, sh2, sc2, g2, ln2_g[l], ln2_b[l], peer_w_q[l].T.astype(BF16), peer_k1[l].astype(BF16),
                 peer_k2[l].astype(BF16), peer_u[l].astype(BF16), peer_v[l].T.astype(BF16), ctx_blocks=cb)

    return X[:, lc:]
```

```python
import functools
import math

import numpy as np
import jax
import jax.numpy as jnp
from jax import lax
from jax.experimental import pallas as pl
from jax.experimental.pallas import tpu as pltpu

F32 = jnp.float32
BF16 = jnp.bfloat16

DEPTH = 4
GRID_W = 64
ROPE_BASE = 10000.0
N_MOD = 6

MLA_HEADS = 8
MLA_Q_RANK = 256
MLA_KV_RANK = 128
MLA_NOPE = 64
MLA_ROPE = 32
MLA_V = 64
MLA_SCALE = (MLA_NOPE + MLA_ROPE) ** -0.5

RET_HEADS = 8
RET_DK = 64
RET_DV = 64
RET_CHUNK = 128

DN_HEADS = 8
DN_DK = 128
DN_DV = 128
DN_CONV = 5
DN_CHUNK = 64

PEER_HEADS = 8
PEER_KEYS = 128
PEER_QDIM = 256
PEER_TOPK = 16

DEEPNORM_ALPHA = (2 * DEPTH) ** 0.25

LANE = 128
VMEM_LIMIT = 56 * 1024 * 1024

NT = (((1,), (1,)), ((), ()))
TN = (((0,), (0,)), ((), ()))


def _cparams(sem):
    return pltpu.CompilerParams(dimension_semantics=sem, vmem_limit_bytes=VMEM_LIMIT)


def _ada_kernel(c_ref, w_ref, b_ref, o_ref):
    c = c_ref[...]
    a = (c * jax.nn.sigmoid(c)).astype(BF16)
    o_ref[...] = jnp.dot(a, w_ref[...].astype(BF16), preferred_element_type=F32) + b_ref[...]


def ada_all(cc, ada_w, ada_b, tn=1024):
    depth, d, n = ada_w.shape
    m = cc.shape[0]
    return pl.pallas_call(
        _ada_kernel,
        grid=(depth, n // tn),
        in_specs=[
            pl.BlockSpec((m, d), lambda l, j: (0, 0)),
            pl.BlockSpec((None, d, tn), lambda l, j: (l, 0, j)),
            pl.BlockSpec((None, 1, tn), lambda l, j: (l, 0, j)),
        ],
        out_specs=pl.BlockSpec((None, m, tn), lambda l, j: (l, 0, j)),
        out_shape=jax.ShapeDtypeStruct((depth, m, n), F32),
        compiler_params=_cparams(("arbitrary", "arbitrary")),
        name="ada",
    )(cc, ada_w, ada_b.reshape(depth, 1, n))


def _resid_ln(x, y, gate, g, b, eps):
    v = DEEPNORM_ALPHA * x + gate * y
    vc = v - jnp.mean(v, axis=-1, keepdims=True)
    var = jnp.mean(vc * vc, axis=-1, keepdims=True)
    return vc * lax.rsqrt(var + eps) * g + b


def _proj_kernel(*refs, mode, splits, eps, group, resid, ln_eps):
    n_in = {"mod": 4, "rms": 3, "gated_rms": 5, "none": 2}[mode]
    ins, rest = refs[:n_in], refs[n_in:]
    w_ref = ins[-1]
    if mode == "mod":
        x_ref, sh_ref, sc_ref = ins[:3]
        x = x_ref[...] * (1.0 + sc_ref[...]) + sh_ref[...]
    elif mode == "rms":
        x_ref, g_ref = ins[:2]
        x = x_ref[...]
        x = x * lax.rsqrt(jnp.mean(x * x, axis=-1, keepdims=True) + eps) * g_ref[...]
    elif mode == "gated_rms":
        a_ref, b_ref, gate_ref, g_ref = ins[:4]
        o = a_ref[...] + b_ref[...]
        gate = gate_ref[...]
        parts = []
        for h in range(o.shape[1] // group):
            oh = o[:, h * group:(h + 1) * group]
            parts.append(oh * lax.rsqrt(jnp.mean(oh * oh, axis=-1, keepdims=True) + eps))
        x = jnp.concatenate(parts, axis=1) * g_ref[...] * (gate * jax.nn.sigmoid(gate))
    else:
        x = ins[0][...]
    z = jnp.dot(x.astype(BF16), w_ref[...], preferred_element_type=F32)
    if resid:
        xs_ref, gt_ref, lg_ref, lb_ref, o_ref = rest
        o_ref[...] = _resid_ln(xs_ref[...], z, gt_ref[...], lg_ref[...], lb_ref[...], ln_eps)
        return
    for o_ref, (a, b) in zip(rest, splits):
        o_ref[...] = z[:, a:b].astype(o_ref.dtype)


def proj(x, w, splits, *, mode="none", shift=None, scale=None, gain=None, other=None, gate=None, group=LANE,
         resid=None, ctx_blocks=1, tm=256, eps=1e-6, ln_eps=1e-5, name="proj"):
    bsz, t, k = x.shape
    n = w.shape[1]
    row = lambda b, j: (b, j, 0)
    sel = lambda b, j: (b, jnp.minimum(j // ctx_blocks, 1), 0, 0)
    vec = lambda b, j: (0, 0)
    in_specs = [pl.BlockSpec((None, tm, k), row)]
    args = [x]
    if mode == "mod":
        in_specs += [pl.BlockSpec((None, None, 1, k), sel)] * 2
        args += [shift, scale]
    elif mode == "rms":
        in_specs += [pl.BlockSpec((1, k), vec)]
        args += [gain.reshape(1, k)]
    elif mode == "gated_rms":
        in_specs += [pl.BlockSpec((None, tm, k), row)] * 2 + [pl.BlockSpec((1, k), vec)]
        args += [other, gate, jnp.tile(gain, k // group).reshape(1, k)]
    in_specs += [pl.BlockSpec((k, n), vec)]
    args += [w]
    if resid is not None:
        stream, gate_sel, ln_g, ln_b = resid
        assert list(splits) == [(0, n)] and stream.shape[-1] == n
        in_specs += [pl.BlockSpec((None, tm, n), row), pl.BlockSpec((None, None, 1, n), sel),
                     pl.BlockSpec((1, n), vec), pl.BlockSpec((1, n), vec)]
        args += [stream, gate_sel, ln_g.reshape(1, n), ln_b.reshape(1, n)]
    return pl.pallas_call(
        functools.partial(_proj_kernel, mode=mode, splits=tuple(splits), eps=eps, group=group,
                          resid=resid is not None, ln_eps=ln_eps),
        grid=(bsz, t // tm),
        in_specs=in_specs,
        out_specs=[pl.BlockSpec((None, tm, b - a), row) for a, b in splits],
        out_shape=[jax.ShapeDtypeStruct((bsz, t, b - a), F32) for a, b in splits],
        compiler_params=_cparams(("parallel", "parallel")),
        name=name,
    )(*args)


def _attn_kernel(q_ref, k_ref, v_ref, o_ref, *, tk, nk, hb):
    tq = q_ref.shape[1]
    dv = v_ref.shape[-1]
    qs = [q_ref[h] for h in range(hb)]

    def body(i, carry):
        start = pl.multiple_of(i * tk, tk)
        ss = [lax.dot_general(qs[h], k_ref[h, pl.ds(start, tk), :], NT, preferred_element_type=F32)
              for h in range(hb)]
        ps, new = [], []
        for h in range(hb):
            m, l, acc = carry[h]
            m_new = jnp.maximum(m, jnp.max(ss[h], axis=1, keepdims=True))
            p = jnp.exp(ss[h] - m_new)
            alpha = jnp.exp(m - m_new)
            ps.append(p.astype(BF16))
            new.append((m_new, alpha * l + jnp.sum(p, axis=1, keepdims=True), alpha * acc))
        out = []
        for h in range(hb):
            m_new, l, acc = new[h]
            out.append((m_new, l, acc + jnp.dot(ps[h], v_ref[h, pl.ds(start, tk), :], preferred_element_type=F32)))
        return tuple(out)

    init = tuple((jnp.full((tq, 1), -jnp.inf, F32), jnp.zeros((tq, 1), F32), jnp.zeros((tq, dv), F32))
                 for _ in range(hb))
    fin = lax.fori_loop(0, nk, body, init)
    for h in range(hb):
        _, l, acc = fin[h]
        o_ref[h] = acc / l


def attention(q, k, v, *, tq, tk, hb=2):
    bsz, h, lq, dq = q.shape
    lk, dv = v.shape[2], v.shape[3]
    assert lq % tq == 0 and lk % tk == 0 and h % hb == 0
    return pl.pallas_call(
        functools.partial(_attn_kernel, tk=tk, nk=lk // tk, hb=hb),
        grid=(bsz, h // hb, lq // tq),
        in_specs=[pl.BlockSpec((None, hb, tq, dq), lambda b, hh, i: (b, hh, i, 0)),
                  pl.BlockSpec((None, hb, lk, dq), lambda b, hh, i: (b, hh, 0, 0)),
                  pl.BlockSpec((None, hb, lk, dv), lambda b, hh, i: (b, hh, 0, 0))],
        out_specs=pl.BlockSpec((None, hb, tq, dv), lambda b, hh, i: (b, hh, i, 0)),
        out_shape=jax.ShapeDtypeStruct((bsz, h, lq, dv), F32),
        compiler_params=_cparams(("parallel", "parallel", "parallel")),
        name="mla_attn",
    )(q, k, v)


def _top_values(s, k):
    vals = []
    cur = s
    for i in range(k):
        m = jnp.max(cur, axis=0, keepdims=True)
        vals.append(m)
        if i + 1 < k:
            cur = jnp.where(cur >= m, -jnp.inf, cur)
    return vals


def _gelu_exact(x):
    return 0.5 * x * (1.0 + lax.erf(x * (2.0 ** -0.5)))


def _peer_prepare(x_ref, sh_ref, sc_ref, wqt_ref, k1_ref, k2_ref, xb_ref, thr_ref, s2_ref, e1_ref, e2_ref):
    half = PEER_QDIM // 2
    K = PEER_TOPK
    H = PEER_HEADS
    x = x_ref[...] * (1.0 + sc_ref[...]) + sh_ref[...]
    xb = x.astype(BF16)
    xb_ref[...] = xb
    qt = lax.dot_general(wqt_ref[...], xb, NT, preferred_element_type=F32)
    v1s, v2s = [], []
    for h in range(H):
        q1 = qt[h * PEER_QDIM: h * PEER_QDIM + half].astype(BF16)
        q2 = qt[h * PEER_QDIM + half: (h + 1) * PEER_QDIM].astype(BF16)
        s1 = jnp.dot(k1_ref[h], q1, preferred_element_type=F32)
        s2 = jnp.dot(k2_ref[h], q2, preferred_element_type=F32)
        thr_ref[h] = s1
        s2_ref[h] = s2
        v1s.append(_top_values(s1, K + 1))
        v2s.append(_top_values(s2, K + 1))
    v1 = [jnp.concatenate([v1s[h][a] for h in range(H)], axis=0) for a in range(K + 1)]
    v2 = [jnp.concatenate([v2s[h][b] for h in range(H)], axis=0) for b in range(K + 1)]
    cands = [v1[a] + v2[b] for a in range(K + 1) for b in range(K + 1) if (a + 1) * (b + 1) <= K + 1]
    cur = list(cands)
    for i in range(K + 1):
        m = functools.reduce(jnp.maximum, cur)
        if i == K - 1:
            kth = m
        if i < K:
            cur = [jnp.where(c >= m, -jnp.inf, c) for c in cur]
    tau = 0.5 * (kth + m)
    top = cands[0]
    z = functools.reduce(jnp.add, [jnp.where(c >= tau, jnp.exp(c - top), 0.0) for c in cands])
    for h in range(H):
        s1 = thr_ref[h]
        thr_ref[h] = tau[h:h + 1] - s1
        e1_ref[h] = jnp.exp(s1 - v1s[h][0]) / z[h:h + 1]
        e2_ref[h] = jnp.exp(s2_ref[h] - v2s[h][0])


def _peer_weights(blk, rows, row_lo, row_hi, thr_ref, s2_ref, e1_ref, e2_ref, ht_ref, a_ref):
    tb = ht_ref.shape[1]
    r0 = blk * rows
    for r in range(row_lo, row_hi):
        rs = slice(r * PEER_KEYS, (r + 1) * PEER_KEYS)
        thr_rows = [thr_ref[h, pl.ds(r0 + r, 1), :] for h in range(PEER_HEADS)]
        e1_rows = [e1_ref[h, pl.ds(r0 + r, 1), :] for h in range(PEER_HEADS)]
        for lt in range(tb // LANE):
            ls = slice(lt * LANE, (lt + 1) * LANE)
            w = None
            for h in range(PEER_HEADS):
                term = jnp.where(s2_ref[h, :, ls] >= thr_rows[h][:, ls], e1_rows[h][:, ls] * e2_ref[h, :, ls], 0.0)
                w = term if w is None else w + term
            a_ref[rs, ls] = (w * _gelu_exact(ht_ref[rs, ls])).astype(BF16)


def _peer_kernel(x_ref, sh_ref, sc_ref, gt_ref, lg_ref, lb_ref, wqt_ref, k1_ref, k2_ref,
                 u0_ref, ua_ref, ub_ref, vta_ref, vtb_ref, o_ref,
                 xb_ref, thr_ref, s2_ref, e1_ref, e2_ref, hte_ref, hto_ref, ae_ref, ao_ref, acc_ref, *, rows, ln_eps):
    g = pl.program_id(2)
    last = pl.num_programs(2) - 1
    p = g % 2
    q = 1 - p
    tabs = (thr_ref, s2_ref, e1_ref, e2_ref)

    @pl.when(g == 0)
    def _first():
        _peer_prepare(x_ref, sh_ref, sc_ref, wqt_ref, k1_ref, k2_ref, xb_ref, *tabs)
        acc_ref[...] = jnp.zeros_like(acc_ref)
        ao_ref[1] = jnp.zeros(ao_ref.shape[1:], ao_ref.dtype)
        hte_ref[0] = lax.dot_general(u0_ref[...], xb_ref[...], NT, preferred_element_type=F32)

    @pl.when(g < last)
    def _even():
        hto_ref[...] = lax.dot_general(ua_ref[...], xb_ref[...], NT, preferred_element_type=F32)
        acc_ref[...] += jnp.dot(vta_ref[...], ao_ref[q], preferred_element_type=F32)
        _peer_weights(2 * g, rows, 0, rows, *tabs, hte_ref.at[p], ae_ref)

    @pl.when(g < last)
    def _odd():
        hte_ref[q] = lax.dot_general(ub_ref[...], xb_ref[...], NT, preferred_element_type=F32)
        acc_ref[...] += jnp.dot(vtb_ref[...], ae_ref[...], preferred_element_type=F32)
        _peer_weights(2 * g + 1, rows, 0, rows, *tabs, hto_ref, ao_ref.at[p])

    @pl.when(g == last)
    def _finish():
        acc = acc_ref[...] + jnp.dot(vta_ref[...], ao_ref[q], preferred_element_type=F32)
        o_ref[...] = _resid_ln(x_ref[...], acc.T, gt_ref[...], lg_ref[...], lb_ref[...], ln_eps)


def peer(x, shift, scale, gate, ln_g, ln_b, wqt, k1, k2, u, vt, *, ctx_blocks, tb=256, eb=1024, ln_eps=1e-5):
    bsz, t, d = x.shape
    n = u.shape[0]
    ne = n // eb
    assert ne % 2 == 0
    rows = eb // PEER_KEYS
    row = lambda b, j, g: (b, j, 0)
    sel = lambda b, j, g: (b, jnp.minimum(j // ctx_blocks, 1), 0, 0)
    const2 = lambda b, j, g: (0, 0)
    const3 = lambda b, j, g: (0, 0, 0)
    tab = pltpu.VMEM((PEER_HEADS, PEER_KEYS, tb), F32)
    ublk = lambda f: pl.BlockSpec((eb, d), lambda b, j, g: (f(g), 0))
    vblk = lambda f: pl.BlockSpec((d, eb), lambda b, j, g: (0, f(g)))
    return pl.pallas_call(
        functools.partial(_peer_kernel, rows=rows, ln_eps=ln_eps),
        grid=(bsz, t // tb, ne // 2 + 1),
        in_specs=[pl.BlockSpec((None, tb, d), row),
                  pl.BlockSpec((None, None, 1, d), sel), pl.BlockSpec((None, None, 1, d), sel),
                  pl.BlockSpec((None, None, 1, d), sel), pl.BlockSpec((1, d), const2), pl.BlockSpec((1, d), const2),
                  pl.BlockSpec(wqt.shape, const2),
                  pl.BlockSpec(k1.shape, const3), pl.BlockSpec(k2.shape, const3),
                  ublk(lambda g: 0),
                  ublk(lambda g: jnp.minimum(2 * g + 1, ne - 1)),
                  ublk(lambda g: jnp.minimum(2 * g + 2, ne - 1)),
                  vblk(lambda g: jnp.maximum(2 * g - 1, 0)),
                  vblk(lambda g: jnp.minimum(2 * g, ne - 1))],
        out_specs=pl.BlockSpec((None, tb, d), row),
        out_shape=jax.ShapeDtypeStruct((bsz, t, d), F32),
        scratch_shapes=[pltpu.VMEM((tb, d), BF16), tab, tab, tab, tab,
                        pltpu.VMEM((2, eb, tb), F32), pltpu.VMEM((eb, tb), F32),
                        pltpu.VMEM((eb, tb), BF16), pltpu.VMEM((2, eb, tb), BF16),
                        pltpu.VMEM((d, tb), F32)],
        compiler_params=_cparams(("parallel", "parallel", "arbitrary")),
        name="peer",
    )(x, shift, scale, gate, ln_g.reshape(1, d), ln_b.reshape(1, d), wqt, k1, k2, u, u, u, vt, vt)


def _bwd_seg(s, cs, nseg):
    return jnp.where(s < cs, cs - 1 - s, nseg - 1 - (s - cs))


def _ret_kernel(qf_ref, kf_ref, vf_ref, qb_ref, kb_ref, vb_ref, d_ref, qdec_ref, kdec_ref, cdec_ref,
                of_ref, ob_ref, s_ref, *, chunk, nchunk, hb):
    @pl.when(pl.program_id(2) == 0)
    def _init():
        s_ref[...] = jnp.zeros_like(s_ref)

    ins = ((qf_ref, kf_ref, vf_ref, of_ref), (qb_ref, kb_ref, vb_ref, ob_ref))
    chains = [(d, j) for d in range(2) for j in range(hb)]

    def step(i, carry):
        sls = [pl.ds(pl.multiple_of((i if d == 0 else nchunk - 1 - i) * chunk, chunk), chunk) for d, _ in chains]
        qs = [ins[d][0][j, sl, :] for (d, j), sl in zip(chains, sls)]
        ks = [ins[d][1][j, sl, :] for (d, j), sl in zip(chains, sls)]
        vs = [ins[d][2][j, sl, :] for (d, j), sl in zip(chains, sls)]
        scores = [lax.dot_general(q, k, NT, preferred_element_type=F32) * d_ref[d, j]
                  for (d, j), q, k in zip(chains, qs, ks)]
        inter = [jnp.dot((q.astype(F32) * qdec_ref[d, j]).astype(BF16), s.astype(BF16), preferred_element_type=F32)
                 for (d, j), q, s in zip(chains, qs, carry)]
        upd = [lax.dot_general((k.astype(F32) * kdec_ref[d, j]).astype(BF16), v, TN, preferred_element_type=F32)
               for (d, j), k, v in zip(chains, ks, vs)]
        for (d, j), sl, sc, v, it in zip(chains, sls, scores, vs, inter):
            ins[d][3][j, sl, :] = jnp.dot(sc.astype(BF16), v, preferred_element_type=F32) + it
        return tuple(s * cdec_ref[j] + u for (d, j), s, u in zip(chains, carry, upd))

    final = lax.fori_loop(0, nchunk, step, tuple(s_ref[d, j] for d, j in chains))
    for (d, j), s in zip(chains, final):
        s_ref[d, j] = s


def retention_scan(q, k, v, dmat, qdec, kdec, cdec, *, chunk, seg, ctx_segs, hb=4):
    bsz, h, t, dk = q.shape
    dv = v.shape[-1]
    assert t % seg == 0 and seg % chunk == 0 and h % hb == 0
    nseg = t // seg
    fwd = lambda d_: pl.BlockSpec((None, hb, seg, d_), lambda b, hh, s: (b, hh, s, 0))
    bwd = lambda d_: pl.BlockSpec((None, hb, seg, d_), lambda b, hh, s: (b, hh, _bwd_seg(s, ctx_segs, nseg), 0))
    per_head = lambda *shape: pl.BlockSpec((2, hb) + shape, lambda b, hh, s: (0, hh, 0, 0))
    return pl.pallas_call(
        functools.partial(_ret_kernel, chunk=chunk, nchunk=seg // chunk, hb=hb),
        grid=(bsz, h // hb, nseg),
        in_specs=[fwd(dk), fwd(dk), fwd(dv), bwd(dk), bwd(dk), bwd(dv),
                  per_head(chunk, chunk), per_head(chunk, dk), per_head(chunk, dk),
                  pl.BlockSpec((hb, dk, dv), lambda b, hh, s: (hh, 0, 0))],
        out_specs=[fwd(dv), bwd(dv)],
        out_shape=[jax.ShapeDtypeStruct((bsz, h, t, dv), F32)] * 2,
        scratch_shapes=[pltpu.VMEM((2, hb, dk, dv), F32)],
        compiler_params=_cparams(("parallel", "parallel", "arbitrary")),
        name="retention",
    )(q, k, v, q, k, v, dmat, qdec, kdec, cdec)


def _dn_kernel(qf_ref, kf_ref, vf_ref, qb_ref, kb_ref, vb_ref, aux_ref, rows_ref, of_ref, ob_ref,
               s_ref, m_s, b_s, qp_s, cd_s, sall_s, *, chunk, nchunk, hb, eps):
    C = chunk

    @pl.when(pl.program_id(2) == 0)
    def _init():
        s_ref[...] = jnp.zeros_like(s_ref)

    ri = lax.broadcasted_iota(jnp.int32, (C, C), 0)
    ci = lax.broadcasted_iota(jnp.int32, (C, C), 1)
    eye = (ri == ci).astype(F32)
    bmm = lambda a, b: jnp.einsum('nij,njk->nik', a.astype(BF16), b.astype(BF16), preferred_element_type=F32)
    bmm_nt = lambda a, b: jnp.einsum('nid,njd->nij', a.astype(BF16), b.astype(BF16),
                                     preferred_element_type=F32)
    bmm_tn = lambda a, b: jnp.einsum('nci,ncj->nij', a.astype(BF16), b.astype(BF16),
                                     preferred_element_type=F32)
    ins = ((qf_ref, kf_ref, vf_ref, of_ref), (qb_ref, kb_ref, vb_ref, ob_ref))
    dk = qf_ref.shape[-1] // hb
    dv = vf_ref.shape[-1] // hb
    chains = [(d, j) for d in range(2) for j in range(hb)]
    nb = len(chains) * nchunk

    def gather(fn):
        return jnp.concatenate([fn(d, j) for d, j in chains], axis=0)

    def l2n(ref, j, scale):
        t = ref[:, j * dk:(j + 1) * dk]
        return (t * (lax.rsqrt(jnp.sum(t * t, axis=-1, keepdims=True) + eps) * scale)).reshape(nchunk, C, dk)

    q = gather(lambda d, j: l2n(ins[d][0], j, dk ** -0.5))
    k = gather(lambda d, j: l2n(ins[d][1], j, 1.0))
    v = gather(lambda d, j: ins[d][2][:, j * dv:(j + 1) * dv].reshape(nchunk, C, dv))
    aux = gather(lambda d, j: aux_ref[d, j].reshape(nchunk, C, aux_ref.shape[-1]))
    rws = gather(lambda d, j: rows_ref[d, j])
    gcol = aux[:, :, 0:1]
    bcol = aux[:, :, 1:2]
    grow = rws[:, 0:1, :]
    brow = rws[:, 1:2, :]
    is_fwd = lax.broadcasted_iota(jnp.int32, (nb, 1, 1), 0) < hb * nchunk
    signed = (ri - ci) * jnp.where(is_fwd, 1, -1)
    incl = signed >= 0
    strict = signed > 0
    kk = bmm_nt(k, k)
    qk = bmm_nt(q, k)
    decay = jnp.where(incl, jnp.exp(jnp.where(incl, gcol - grow, 0.0)), 0.0)
    x = jnp.where(strict, -(kk * decay * bcol), 0.0)
    x2 = bmm(x, x)
    x4 = bmm(x2, x2)
    x8 = bmm(x4, x4)
    x16 = bmm(x8, x8)
    x32 = bmm(x16, x16)
    p1 = eye + x + x2 + bmm(x, x2)
    p2 = eye + x4 + x8 + bmm(x4, x8)
    p3 = eye + x16 + x32 + bmm(x16, x32)
    tinv = bmm(bmm(p1, p2), p3)
    egrow = jnp.exp(grow)
    u = bmm(tinv * brow, v)
    w = bmm(tinv * (brow * egrow), k)
    attn = qk * decay
    gtot = jnp.where(is_fwd, grow[:, :, C - 1:C], grow[:, :, 0:1])
    qd = q * jnp.exp(gcol)
    kd = k * jnp.exp(gtot - gcol)
    per_chain = lambda t: t.reshape((2, hb, nchunk) + t.shape[1:])
    m_s[...] = per_chain(bmm_tn(kd, w).astype(BF16))
    b_s[...] = per_chain(bmm_tn(kd, u))
    qp_s[...] = per_chain((qd - bmm(attn, w)).astype(BF16))
    cd_s[...] = per_chain(jnp.broadcast_to(jnp.exp(gtot), (nb, 1, dv)))
    o_local = bmm(attn, u)

    def step(i, carry):
        new = []
        for (d, j), s in zip(chains, carry):
            c = i if d == 0 else nchunk - 1 - i
            sb = s.astype(BF16)
            sall_s[d, j, c] = sb
            new.append(s * cd_s[d, j, c] - jnp.dot(m_s[d, j, c], sb, preferred_element_type=F32) + b_s[d, j, c])
        return tuple(new)

    final = lax.fori_loop(0, nchunk, step, tuple(s_ref[d, j] for d, j in chains))
    for (d, j), s in zip(chains, final):
        s_ref[d, j] = s

    o_all = o_local + bmm(qp_s[...].reshape(nb, C, dk), sall_s[...].reshape(nb, dk, dv))
    for idx, (d, j) in enumerate(chains):
        ins[d][3][:, j * dv:(j + 1) * dv] = o_all[idx * nchunk:(idx + 1) * nchunk].reshape(nchunk * C, dv)


def deltanet_scan(qkv, aux, rows, *, heads, dk, dv, chunk, seg, ctx_segs, hb=4, eps=1e-6):
    bsz, t, _ = qkv.shape
    assert dk == dv == LANE and t % seg == 0 and seg % chunk == 0 and heads % hb == 0
    nseg = t // seg
    nchunk = seg // chunk
    hg = heads // hb
    fwd = lambda off: pl.BlockSpec((None, seg, hb * dk), lambda b, hh, s: (b, s, off + hh))
    bwd = lambda off: pl.BlockSpec((None, seg, hb * dk), lambda b, hh, s: (b, _bwd_seg(s, ctx_segs, nseg), off + hh))
    return pl.pallas_call(
        functools.partial(_dn_kernel, chunk=chunk, nchunk=nchunk, hb=hb, eps=eps),
        grid=(bsz, hg, nseg),
        in_specs=[fwd(0), fwd(hg), fwd(2 * hg), bwd(0), bwd(hg), bwd(2 * hg),
                  pl.BlockSpec((2, None, hb, seg, aux.shape[-1]), lambda b, hh, s: (0, b, hh, s, 0)),
                  pl.BlockSpec((2, None, hb, nchunk, 8, chunk), lambda b, hh, s: (0, b, hh, s, 0, 0))],
        out_specs=[fwd(0), bwd(0)],
        out_shape=[jax.ShapeDtypeStruct((bsz, t, heads * dv), F32)] * 2,
        scratch_shapes=[pltpu.VMEM((2, hb, dk, dv), F32),
                        pltpu.VMEM((2, hb, nchunk, dk, dv), BF16),
                        pltpu.VMEM((2, hb, nchunk, dk, dv), F32),
                        pltpu.VMEM((2, hb, nchunk, chunk, dk), BF16),
                        pltpu.VMEM((2, hb, nchunk, 1, dv), F32),
                        pltpu.VMEM((2, hb, nchunk, dk, dv), BF16)],
        compiler_params=_cparams(("parallel", "parallel", "arbitrary")),
        name="deltanet",
    )(qkv, qkv, qkv, qkv, qkv, qkv, aux, rows)


def _rope_tables(length, dim, ctx_len):
    rows = length // GRID_W
    n_freq = dim // 4
    inv = ROPE_BASE ** (-jnp.arange(n_freq, dtype=F32) / n_freq)
    r = jnp.repeat(jnp.arange(rows, dtype=F32), GRID_W)
    c = jnp.tile(jnp.arange(GRID_W, dtype=F32), rows)
    ang = jnp.concatenate([r[:, None] * inv, c[:, None] * inv], axis=-1)
    cos = jnp.concatenate([jnp.ones((ctx_len, dim // 2), F32), jnp.cos(ang)], axis=0)
    sin = jnp.concatenate([jnp.zeros((ctx_len, dim // 2), F32), jnp.sin(ang)], axis=0)
    return cos, sin


def _apply_rope(x, cos, sin):
    x1, x2 = jnp.split(x, 2, axis=-1)
    return jnp.concatenate([x1 * cos - x2 * sin, x2 * cos + x1 * sin], axis=-1)


def _retention(rq, rk, rv, cos, sin, lc, seg):
    b, t, _ = rq.shape
    q = _apply_rope(rq.reshape(b, t, RET_HEADS, RET_DK), cos[:, None], sin[:, None])
    k = _apply_rope(rk.reshape(b, t, RET_HEADS, RET_DK), cos[:, None], sin[:, None]) * RET_DK ** -0.5
    v = rv.reshape(b, t, RET_HEADS, RET_DV)
    tr = lambda a: jnp.transpose(a, (0, 2, 1, 3)).astype(BF16)
    lg = jnp.log1p(-jnp.exp2(-5.0 - jnp.arange(RET_HEADS, dtype=F32)))[:, None, None]
    C = RET_CHUNK
    pos = jnp.arange(C, dtype=F32)
    dist = pos[:, None] - pos[None, :]
    dm = lambda dd, mask: jnp.where(mask, jnp.exp(jnp.where(mask, dd, 0.0) * lg), 0.0)
    dmat = jnp.stack([dm(dist, dist >= 0), dm(-dist, dist < 0)])
    col = lambda e: jnp.broadcast_to(jnp.exp(e * lg[:, :, 0])[..., None], (RET_HEADS, C, RET_DK))
    qdec = jnp.stack([col(pos + 1.0), col(C - pos)])
    kdec = jnp.stack([col(C - 1.0 - pos), col(pos)])
    cdec = jnp.broadcast_to(jnp.exp(C * lg), (RET_HEADS, RET_DK, RET_DV))
    o_f, o_b = retention_scan(tr(q), tr(k), tr(v), dmat, qdec, kdec, cdec, chunk=C, seg=seg, ctx_segs=lc // seg)
    return o_f + o_b


def _group_norm_heads(o, g, eps=1e-5):
    oc = o - jnp.mean(o, -1, keepdims=True)
    y = oc * lax.rsqrt(jnp.mean(oc * oc, -1, keepdims=True) + eps)
    b, h, L, dv = o.shape
    return jnp.transpose(y, (0, 2, 1, 3)).reshape(b, L, h * dv) * g


def _conv_silu(x, w, lc):
    b, t, ch = x.shape
    K = w.shape[0]
    half = (K - 1) // 2
    xp = jnp.pad(x, ((0, 0), (half, K // 2), (0, 0)))
    pos = jnp.arange(t)
    acc = 0.0
    for j in range(K):
        src = pos + (j - half)
        ok = (src >= 0) & (src < t) & ((pos < lc) == (src < lc))
        acc = acc + jnp.where(ok[None, :, None], xp[:, j:j + t], 0.0) * w[j]
    return acc * jax.nn.sigmoid(acc)


def _deltanet(qkv, ab, conv_w, a_log, dt_bias, lc, seg):
    b, t, _ = qkv.shape
    H, C = DN_HEADS, DN_CHUNK
    act = _conv_silu(qkv, conv_w, lc)
    a = ab[..., :2 * H].reshape(b, t, 2, H)
    bt = ab[..., 2 * H:4 * H].reshape(b, t, 2, H)
    g = jnp.transpose(-jnp.exp(a_log) * jax.nn.softplus(a + dt_bias), (2, 0, 3, 1))
    beta = jnp.transpose(jax.nn.sigmoid(bt), (2, 0, 3, 1))
    gch = g.reshape(2, b, H, t // C, C)
    pre = jnp.cumsum(gch, axis=-1)
    gc = jnp.stack([pre[0], jnp.sum(gch[1], -1, keepdims=True) - pre[1] + gch[1]])
    bc = beta.reshape(2, b, H, t // C, C)
    rows = jnp.pad(jnp.stack([gc, bc], axis=-2), ((0, 0),) * 4 + ((0, 6), (0, 0)))
    aux = jnp.pad(jnp.stack([gc.reshape(2, b, H, t), beta], axis=-1), ((0, 0),) * 4 + ((0, 6),))
    nseg, cs = t // seg, lc // seg

    def visit_order(a_, per_seg):
        shp = a_.shape
        a_ = a_.reshape(shp[:2] + (nseg, per_seg) + shp[3:])
        a_ = jnp.concatenate([jnp.flip(a_[:, :, :cs], 2), jnp.flip(a_[:, :, cs:], 2)], axis=2)
        return a_.reshape(shp)

    rows = jnp.stack([rows[0], visit_order(rows[1], seg // C)])
    aux = jnp.stack([aux[0], visit_order(aux[1], seg)])
    return deltanet_scan(act, aux, rows, heads=H, dk=DN_DK, dv=DN_DV, chunk=C, seg=seg, ctx_segs=cs)


def _pad_cols(w, width):
    return jnp.pad(w, ((0, 0), (0, width - w.shape[1])))


def kernel(x, c, ctx, c_ctx, ada_w, ada_b, ln1_g, ln1_b, ln2_g, ln2_b, ar_w_in, mla_q_norm, mla_w_uq,
           mla_kv_norm, mla_w_ukv, ret_gn_g, ar_w_out, dn_w_in, dn_conv, dn_a_log, dn_dt_bias, dn_norm_g,
           dn_w_out, peer_w_q, peer_k1, peer_k2, peer_u, peer_v):
    bsz, L, D = x.shape
    lc = ctx.shape[1]
    T = lc + L
    tm = 256
    assert lc % tm == 0 and L % tm == 0
    cb = lc // tm
    seg = tm

    X = jnp.concatenate([ctx, x], axis=1)
    cc = jnp.zeros((8, D), F32).at[:bsz].set(c).at[bsz].set(c_ctx)
    mod_all = ada_all(cc, ada_w, ada_b)

    cos_m, sin_m = _rope_tables(L, MLA_ROPE, lc)
    cos_r, sin_r = _rope_tables(L, RET_DK, lc)

    for l in range(DEPTH):
        j = l // 2
        mod = mod_all[l]
        ml = mod[:bsz].reshape(bsz, N_MOD, D)
        mc = jnp.broadcast_to(mod[bsz].reshape(1, N_MOD, D), (bsz, N_MOD, D))
        msel = jnp.stack([mc, ml], axis=1)
        mvec = [msel[:, :, i][:, :, None, :] for i in range(N_MOD)]
        sh1, sc1, g1, sh2, sc2, g2 = mvec

        if l % 2 == 0:
            w_in = ar_w_in[j]
            w_pad = jnp.concatenate(
                [w_in[:, :416], jnp.zeros((D, 96), F32), w_in[:, 416:]], axis=1).astype(BF16)
            splits = [(0, 256), (256, 384), (384, 512), (512, 1024), (1024, 1536), (1536, 2048), (2048, 2560)]
            cq, ckv, krp, rq, rk, rv, rg = proj(X, w_pad, splits, mode="mod", shift=sh1, scale=sc1,
                                                ctx_blocks=cb, tm=tm, name="ar_in")
            (qf,) = proj(cq, mla_w_uq[j].astype(BF16), [(0, MLA_HEADS * (MLA_NOPE + MLA_ROPE))], mode="rms",
                         gain=mla_q_norm[j], tm=tm, name="mla_uq")
            (kvf,) = proj(ckv, mla_w_ukv[j].astype(BF16), [(0, MLA_HEADS * (MLA_NOPE + MLA_V))], mode="rms",
                          gain=mla_kv_norm[j], tm=tm, name="mla_ukv")
            qf = qf.reshape(bsz, T, MLA_HEADS, MLA_NOPE + MLA_ROPE)
            kvf = kvf.reshape(bsz, T, MLA_HEADS, MLA_NOPE + MLA_V)
            qn, qr = qf[..., :MLA_NOPE], qf[..., MLA_NOPE:]
            kn, vv = kvf[..., :MLA_NOPE], kvf[..., MLA_NOPE:]
            qr = _apply_rope(qr, cos_m[:, None], sin_m[:, None])
            kr = _apply_rope(krp[..., :MLA_ROPE], cos_m, sin_m)
            qh = jnp.transpose(jnp.concatenate([qn, qr], -1) * MLA_SCALE, (0, 2, 1, 3)).astype(BF16)
            kh = jnp.transpose(jnp.concatenate(
                [kn, jnp.broadcast_to(kr[:, :, None, :], (bsz, T, MLA_HEADS, MLA_ROPE))], -1), (0, 2, 1, 3)).astype(BF16)
            vh = jnp.transpose(vv, (0, 2, 1, 3)).astype(BF16)
            tk = next(c for c in (2816, 1408, 768, lc) if T % c == 0)
            o_l = attention(qh[:, :, lc:], kh, vh, tq=1024 if L % 1024 == 0 else 512, tk=tk)
            o_c = attention(qh[:, :, :lc], kh[:, :, :lc], vh[:, :, :lc], tq=lc, tk=lc)
            mla = jnp.transpose(jnp.concatenate([o_c, o_l], axis=2), (0, 2, 1, 3)).reshape(bsz, T, MLA_HEADS * MLA_V)
            ro = _retention(rq, rk, rv, cos_r, sin_r, lc, seg)
            ret = _group_norm_heads(ro, ret_gn_g[j]) * jax.nn.silu(rg)
            mix = jnp.concatenate([mla, ret], axis=-1)
            (X,) = proj(mix, ar_w_out[j].astype(BF16), [(0, D)], resid=(X, g1, ln1_g[l], ln1_b[l]), ctx_blocks=cb,
                        tm=tm, name="ar_out")
        else:
            w_pad = _pad_cols(dn_w_in[j], 4224).astype(BF16)
            splits = [(0, 3072), (3072, 4096), (4096, 4224)]
            qkv, gate, ab = proj(X, w_pad, splits, mode="mod", shift=sh1, scale=sc1, ctx_blocks=cb, tm=tm,
                                 name="dn_in")
            o_f, o_b = _deltanet(qkv, ab, dn_conv[j], dn_a_log[j], dn_dt_bias[j], lc, seg)
            (X,) = proj(o_f, dn_w_out[j].astype(BF16), [(0, D)], mode="gated_rms", other=o_b, gate=gate,
                        gain=dn_norm_g[j], group=DN_DV, resid=(X, g1, ln1_g[l], ln1_b[l]), ctx_blocks=cb, tm=tm,
                        name="dn_out")

        X = peer(X, sh2, sc2, g2, ln2_g[l], ln2_b[l], peer_w_q[l].T.astype(BF16), peer_k1[l].astype(BF16),
                 peer_k2[l].astype(BF16), peer_u[l].astype(BF16), peer_v[l].T.astype(BF16), ctx_blocks=cb)

    return X[:, lc:]
```
